```python
import math
import jax
import jax.numpy as jnp
from jax import lax
import numpy as np

D_MODEL = 1024
BATCH = 8
SEQ = 8192
DEPTH = 4
DEC_BATCH = 16
DEC_SEQ = 16
PAST_LEN = 1024

CHUNK = 64
N_EVEN = (DEPTH + 1) // 2
N_ODD = DEPTH // 2
EPS = 1e-6
H_ATT = 8
D_HEAD = D_MODEL // H_ATT // 2
D_V = 2 * D_HEAD
D_QK = H_ATT * 2 * D_HEAD
D_ATT = H_ATT * D_V
Q_BLOCK = 128
D_INNER = D_MODEL
SSM_HEADDIM = 64
H_SSM = D_INNER // SSM_HEADDIM
D_STATE = 128
N_GROUPS = 2
SSM_CONV = 4
D_XBC = D_INNER + 2 * N_GROUPS * D_STATE
HYB_SPLITS = (D_QK, 2 * D_QK, 2 * D_QK + D_ATT, 2 * D_QK + D_ATT + D_INNER,
              2 * D_QK + D_ATT + D_INNER + D_XBC)
D_IN_HYB = 2 * D_QK + D_ATT + D_INNER + D_XBC + H_SSM
D_MIX_OUT = D_ATT + D_INNER
CONF_KW = 31
D_FF = -(-8 * D_MODEL // (3 * 256)) * 256

kernel_name = "hybrid_diffattn_ssd_conformer_stream_step"


def _rmsnorm(x, g):
    xf = x.astype(jnp.float32)
    y = xf * lax.rsqrt(jnp.mean(xf * xf, axis=-1, keepdims=True) + EPS)
    return (y * g.astype(jnp.float32)).astype(x.dtype)


def _layernorm(x, g, b):
    xf = x.astype(jnp.float32)
    mu = jnp.mean(xf, axis=-1, keepdims=True)
    var = jnp.mean(jnp.square(xf - mu), axis=-1, keepdims=True)
    y = (xf - mu) * lax.rsqrt(var + EPS) * g.astype(jnp.float32) + b.astype(jnp.float32)
    return y.astype(x.dtype)


def _alibi_slopes():
    return 2.0 ** (-8.0 * jnp.arange(1, H_ATT + 1, dtype=jnp.float32) / H_ATT)


def _causal_dwconv(u, past, w, b):
    full = jnp.concatenate([past.astype(u.dtype), u], axis=1)
    y = lax.conv_general_dilated(full, w[:, None, :].astype(u.dtype), (1,), "VALID",
                                 dimension_numbers=("NWC", "WIO", "NWC"),
                                 feature_group_count=u.shape[-1])
    return y + b.astype(u.dtype), full[:, full.shape[1] - (w.shape[0] - 1):]


def _diff_attn_block(qb, qpos, k, v, kpos, lam, slopes):
    s = jnp.einsum("bqhmd,bkhmd->bhmqk", qb, k).astype(jnp.float32) * (D_HEAD ** -0.5)
    dist = jnp.abs(qpos[:, None] - kpos[None, :]).astype(jnp.float32)
    bias = -slopes[:, None, None] * dist
    visible = (kpos[None, :] // CHUNK) <= (qpos[:, None] // CHUNK)
    s = jnp.where(visible, s + bias[None, :, None], -jnp.inf)
    p = jax.nn.softmax(s, axis=-1)
    a = p[:, :, 0] - lam * p[:, :, 1]
    return jnp.einsum("bhqk,bkhe->bqhe", a.astype(v.dtype), v)


def _ssd_scan(x, dt, a, bm, cm, h0, block):
    B, T, H, P = x.shape
    nc = T // block

    def to_chunks(t):
        return jnp.moveaxis(t.reshape((B, nc, block) + t.shape[2:]), 1, 0)

    xs = (to_chunks(x.astype(jnp.float32)), to_chunks(dt.astype(jnp.float32)),
          to_chunks(bm.astype(jnp.float32)), to_chunks(cm.astype(jnp.float32)))
    causal = jnp.tril(jnp.ones((block, block), dtype=bool))

    def step(h, inp):
        xc, dtc, bc, cc = inp
        bc = jnp.repeat(bc, H // N_GROUPS, axis=2)
        cc = jnp.repeat(cc, H // N_GROUPS, axis=2)
        acum = jnp.cumsum(dtc * a, axis=1)
        seg = jnp.where(causal[None, :, :, None],
                        acum[:, :, None, :] - acum[:, None, :, :], -jnp.inf)
        xdt = xc * dtc[..., None]
        scores = jnp.einsum("blhn,bshn->blsh", cc, bc) * jnp.exp(seg)
        y_diag = jnp.einsum("blsh,bshp->blhp", scores, xdt)
        y_off = jnp.einsum("blhn,bhpn->blhp", cc, h) * jnp.exp(acum)[..., None]
        decay_end = jnp.exp(acum[:, -1:, :] - acum)
        h = h * jnp.exp(acum[:, -1, :])[:, :, None, None] + jnp.einsum(
            "blhn,blhp->bhpn", bc * decay_end[..., None], xdt)
        return h, y_diag + y_off

    h, ys = lax.scan(step, h0.astype(jnp.float32), xs)
    return jnp.moveaxis(ys, 0, 1).reshape(B, T, H, P), h


def _hybrid_mixer(h, k_past, v_past, conv_past, ssm_past, w_in, lam_vec, lam_init, subln_g,
                  conv_w, conv_b, dt_bias, a_log, d_skip, norm_g, w_out):
    B, T, _ = h.shape
    pos0 = 0 if k_past is None else k_past.shape[1]
    q, k, v, z, xbc, dt_raw = jnp.split(h @ w_in, HYB_SPLITS, axis=-1)
    q = q.reshape(B, T, H_ATT, 2, D_HEAD)
    k = k.reshape(B, T, H_ATT, 2, D_HEAD)
    v = v.reshape(B, T, H_ATT, D_V)
    lv = lam_vec.astype(jnp.float32)
    lam = jnp.exp(jnp.sum(lv[0] * lv[1])) - jnp.exp(jnp.sum(lv[2] * lv[3])) + lam_init
    slopes = _alibi_slopes()
    qpos = pos0 + jnp.arange(T, dtype=jnp.int32)
    if k_past is None:
        nblk = T // Q_BLOCK
        q_blocks = jnp.moveaxis(q.reshape(B, nblk, Q_BLOCK, H_ATT, 2, D_HEAD), 1, 0)
        pos_blocks = qpos.reshape(nblk, Q_BLOCK)
        o = lax.map(lambda qp: _diff_attn_block(qp[0], qp[1], k, v, qpos, lam, slopes),
                    (q_blocks, pos_blocks))
        o = jnp.moveaxis(o, 0, 1).reshape(B, T, H_ATT, D_V)
        conv_past = jnp.zeros((B, SSM_CONV - 1, D_XBC), h.dtype)
        ssm_past = jnp.zeros((B, H_SSM, SSM_HEADDIM, D_STATE), jnp.float32)
    else:
        k_all = jnp.concatenate([k_past.reshape(B, pos0, H_ATT, 2, D_HEAD).astype(k.dtype), k], axis=1)
        v_all = jnp.concatenate([v_past.astype(v.dtype), v], axis=1)
        kpos = jnp.arange(pos0 + T, dtype=jnp.int32)
        o = _diff_attn_block(q, qpos, k_all, v_all, kpos, lam, slopes)
    o = (_rmsnorm(o, subln_g) * (1.0 - lam_init)).reshape(B, T, D_ATT)
    xbc, conv_new = _causal_dwconv(xbc, conv_past, conv_w, conv_b)
    xbc = jax.nn.silu(xbc)
    xs, bm, cm = jnp.split(xbc, [D_INNER, D_INNER + N_GROUPS * D_STATE], axis=-1)
    xs = xs.reshape(B, T, H_SSM, SSM_HEADDIM)
    dt = jax.nn.softplus(dt_raw.astype(jnp.float32) + dt_bias.astype(jnp.float32))
    a = -jnp.exp(a_log.astype(jnp.float32))
    y, ssm_new = _ssd_scan(xs, dt, a, bm.reshape(B, T, N_GROUPS, D_STATE),
                           cm.reshape(B, T, N_GROUPS, D_STATE), ssm_past, min(T, CHUNK))
    y = y + d_skip.astype(jnp.float32)[:, None] * xs.astype(jnp.float32)
    y = y.reshape(B, T, D_INNER) * jax.nn.silu(z.astype(jnp.float32))
    y = _rmsnorm(y.reshape(B, T, N_GROUPS, D_INNER // N_GROUPS),
                 norm_g.reshape(N_GROUPS, D_INNER // N_GROUPS)).reshape(B, T, D_INNER).astype(h.dtype)
    out = jnp.concatenate([o, y], axis=-1) @ w_out
    return out, k.reshape(B, T, H_ATT, 2 * D_HEAD), v, conv_new, ssm_new


def _conformer_conv(h, conv_past, w_in, b_in, dw_w, dw_b, ln_g, ln_b, w_out, b_out):
    B = h.shape[0]
    if conv_past is None:
        conv_past = jnp.zeros((B, CONF_KW - 1, D_MODEL), h.dtype)
    u = h @ w_in + b_in
    u = u[..., :D_MODEL] * jax.nn.sigmoid(u[..., D_MODEL:])
    y, conv_new = _causal_dwconv(u, conv_past, dw_w, dw_b)
    y = jax.nn.silu(_layernorm(y, ln_g, ln_b))
    return y @ w_out + b_out, conv_new


def _swiglu(h, w_up, w_down):
    g, u = jnp.split(h @ w_up, 2, axis=-1)
    return (jax.nn.silu(g) * u) @ w_down


def _trunk(x, c, past, weights):
    (ada_w, ada_b, norm_g, ffn_w_up, ffn_w_down,
     hyb_w_in, attn_lambda, attn_subln_g, ssm_conv_w, ssm_conv_b,
     ssm_dt_bias, ssm_a_log, ssm_d, ssm_norm_g, hyb_w_out,
     conf_w_in, conf_b_in, conf_dw_w, conf_dw_b, conf_ln_g, conf_ln_b,
     conf_w_out, conf_b_out) = weights
    mod = jnp.einsum("bd,lde->lbe", jax.nn.silu(c), ada_w) + ada_b[:, None, :]
    ks, vs, sconvs, ssms, cconvs = [], [], [], [], []
    for l in range(DEPTH):
        shift_m, scale_m, gate_m, shift_f, scale_f, gate_f = jnp.split(mod[l][:, None, :], 6, axis=-1)
        h = _rmsnorm(x, norm_g[l, 0]) * (1.0 + scale_m) + shift_m
        j = l // 2
        if l % 2 == 0:
            if past is None:
                k_past = v_past = conv_past = ssm_past = None
            else:
                k_past, v_past, conv_past, ssm_past = past[0][j], past[1][j], past[2][j], past[3][j]
            lam_init = 0.8 - 0.6 * math.exp(-0.3 * l)
            out, k_rows, v_rows, conv_new, ssm_new = _hybrid_mixer(
                h, k_past, v_past, conv_past, ssm_past, hyb_w_in[j], attn_lambda[j], lam_init,
                attn_subln_g[j], ssm_conv_w[j], ssm_conv_b[j], ssm_dt_bias[j], ssm_a_log[j],
                ssm_d[j], ssm_norm_g[j], hyb_w_out[j])
            ks.append(k_rows)
            vs.append(v_rows)
            sconvs.append(conv_new)
            ssms.append(ssm_new)
        else:
            conv_past = None if past is None else past[4][j]
            out, conv_new = _conformer_conv(h, conv_past, conf_w_in[j], conf_b_in[j], conf_dw_w[j],
                                            conf_dw_b[j], conf_ln_g[j], conf_ln_b[j],
                                            conf_w_out[j], conf_b_out[j])
            cconvs.append(conv_new)
        x = x + gate_m * _rmsnorm(out, norm_g[l, 1])
        h = _rmsnorm(x, norm_g[l, 2]) * (1.0 + scale_f) + shift_f
        x = x + gate_f * _rmsnorm(_swiglu(h, ffn_w_up[l], ffn_w_down[l]), norm_g[l, 3])
    return x, jnp.stack(ks), jnp.stack(vs), jnp.stack(sconvs), jnp.stack(ssms), jnp.stack(cconvs)


def setup_inputs(seed: int = 0) -> dict:
    key = jax.random.key(seed)
    ks = iter(jax.random.split(key, 40))

    def nrm(shape, scale):
        return scale * jax.random.normal(next(ks), shape, jnp.float32)

    dt0 = jnp.exp(jax.random.uniform(next(ks), (N_EVEN, H_SSM), jnp.float32,
                                     minval=math.log(1e-3), maxval=math.log(1e-1)))
    return {
        "x_prompt": nrm((BATCH, SEQ, D_MODEL), 1.0),
        "x_sample": nrm((DEC_BATCH, DEC_SEQ, D_MODEL), 1.0),
        "c_prompt": nrm((BATCH, D_MODEL), 1.0),
        "c_sample": nrm((DEC_BATCH, D_MODEL), 1.0),
        "cache_attn_k": nrm((N_EVEN, DEC_BATCH, PAST_LEN, H_ATT, 2 * D_HEAD), 1.0),
        "cache_attn_v": nrm((N_EVEN, DEC_BATCH, PAST_LEN, H_ATT, D_V), 1.0),
        "state_ssm_conv": nrm((N_EVEN, DEC_BATCH, SSM_CONV - 1, D_XBC), 1.0),
        "state_ssm": nrm((N_EVEN, DEC_BATCH, H_SSM, SSM_HEADDIM, D_STATE), 0.3),
        "state_conf_conv": nrm((N_ODD, DEC_BATCH, CONF_KW - 1, D_MODEL), 0.5),
        "ada_w": nrm((DEPTH, D_MODEL, 6 * D_MODEL), D_MODEL ** -0.5),
        "ada_b": nrm((DEPTH, 6 * D_MODEL), 0.01),
        "norm_g": 1.0 + nrm((DEPTH, 4, D_MODEL), 0.05),
        "ffn_w_up": nrm((DEPTH, D_MODEL, 2 * D_FF), D_MODEL ** -0.5),
        "ffn_w_down": nrm((DEPTH, D_FF, D_MODEL), D_FF ** -0.5),
        "hyb_w_in": nrm((N_EVEN, D_MODEL, D_IN_HYB), D_MODEL ** -0.5),
        "attn_lambda": nrm((N_EVEN, 4, D_HEAD), 0.1),
        "attn_subln_g": 1.0 + nrm((N_EVEN, D_V), 0.05),
        "ssm_conv_w": nrm((N_EVEN, SSM_CONV, D_XBC), SSM_CONV ** -0.5),
        "ssm_conv_b": nrm((N_EVEN, D_XBC), 0.01),
        "ssm_dt_bias": dt0 + jnp.log(-jnp.expm1(-dt0)),
        "ssm_a_log": jnp.log(jax.random.uniform(next(ks), (N_EVEN, H_SSM), jnp.float32, minval=1.0, maxval=16.0)),
        "ssm_d": 1.0 + nrm((N_EVEN, H_SSM), 0.1),
        "ssm_norm_g": 1.0 + nrm((N_EVEN, D_INNER), 0.05),
        "hyb_w_out": nrm((N_EVEN, D_MIX_OUT, D_MODEL), D_MIX_OUT ** -0.5),
        "conf_w_in": nrm((N_ODD, D_MODEL, 2 * D_MODEL), D_MODEL ** -0.5),
        "conf_b_in": nrm((N_ODD, 2 * D_MODEL), 0.01),
        "conf_dw_w": nrm((N_ODD, CONF_KW, D_MODEL), CONF_KW ** -0.5),
        "conf_dw_b": nrm((N_ODD, D_MODEL), 0.01),
        "conf_ln_g": 1.0 + nrm((N_ODD, D_MODEL), 0.05),
        "conf_ln_b": nrm((N_ODD, D_MODEL), 0.01),
        "conf_w_out": nrm((N_ODD, D_MODEL, D_MODEL), D_MODEL ** -0.5),
        "conf_b_out": nrm((N_ODD, D_MODEL), 0.01),
    }


def reference(x_prompt, x_sample, c_prompt, c_sample, cache_attn_k, cache_attn_v,
              state_ssm_conv, state_ssm, state_conf_conv, ada_w, ada_b, norm_g, ffn_w_up,
              ffn_w_down, hyb_w_in, attn_lambda, attn_subln_g, ssm_conv_w, ssm_conv_b,
              ssm_dt_bias, ssm_a_log, ssm_d, ssm_norm_g, hyb_w_out, conf_w_in, conf_b_in,
              conf_dw_w, conf_dw_b, conf_ln_g, conf_ln_b, conf_w_out, conf_b_out):
    weights = (ada_w, ada_b, norm_g, ffn_w_up, ffn_w_down,
               hyb_w_in, attn_lambda, attn_subln_g, ssm_conv_w, ssm_conv_b,
               ssm_dt_bias, ssm_a_log, ssm_d, ssm_norm_g, hyb_w_out,
               conf_w_in, conf_b_in, conf_dw_w, conf_dw_b, conf_ln_g, conf_ln_b,
               conf_w_out, conf_b_out)
    y_prompt, k_p, v_p, sconv_p, ssm_p, cconv_p = _trunk(x_prompt, c_prompt, None, weights)
    past = (cache_attn_k, cache_attn_v, state_ssm_conv, state_ssm, state_conf_conv)
    y_sample, k_s, v_s, sconv_s, ssm_s, cconv_s = _trunk(x_sample, c_sample, past, weights)
    return (y_prompt, y_sample, k_p, v_p, sconv_p, ssm_p, cconv_p, k_s, v_s, sconv_s, ssm_s, cconv_s)
```

```python
import functools
import math

import numpy as np
import jax
import jax.numpy as jnp
from jax import lax
from jax.experimental import pallas as pl
from jax.experimental.pallas import tpu as pltpu

F32 = jnp.float32
BF16 = jnp.bfloat16

D_MODEL = 1024
DEPTH = 4
CHUNK = 64
CHUNK_SHIFT = CHUNK.bit_length() - 1
assert 1 << CHUNK_SHIFT == CHUNK
EPS = 1e-6
H_ATT = 8
D_HEAD = D_MODEL // H_ATT // 2
D_V = 2 * D_HEAD
D_QK = H_ATT * 2 * D_HEAD
D_ATT = H_ATT * D_V
D_INNER = D_MODEL
SSM_HEADDIM = 64
H_SSM = D_INNER // SSM_HEADDIM
D_STATE = 128
N_GROUPS = 2
SSM_CONV = 4
D_XBC = D_INNER + 2 * N_GROUPS * D_STATE
CONF_KW = 31
D_FF = -(-8 * D_MODEL // (3 * 256)) * 256

LANES = 128
SUBLANES = 8
VMEM_LIMIT = 56 * 1024 * 1024
NEG = -1e30
QK_SCALE = D_HEAD ** -0.5
D_DT_PAD = LANES
D_ZXD = D_INNER + D_XBC + D_DT_PAD
CONF_HIST = 32


def _cparams(sem):
    return pltpu.CompilerParams(dimension_semantics=sem, vmem_limit_bytes=VMEM_LIMIT)


def _dot(a, b):
    return jnp.dot(a, b, preferred_element_type=F32)


def _dot_nt(a, b):
    return lax.dot_general(a, b, (((1,), (1,)), ((), ())), preferred_element_type=F32)


def _dot_tn(a, b):
    return lax.dot_general(a, b, (((0,), (0,)), ((), ())), preferred_element_type=F32)


def _rms(xf, g):
    return xf * lax.rsqrt(jnp.mean(xf * xf, axis=-1, keepdims=True) + EPS) * g


def _norm_mod(xf, g, sc, sh):
    return _rms(xf, g) * (1.0 + sc) + sh


def _silu(x):
    return x * jax.nn.sigmoid(x)


def _split3(v):
    hi = v.astype(BF16)
    r = v - hi.astype(F32)
    mid = r.astype(BF16)
    lo = (r - mid.astype(F32)).astype(BF16)
    return hi, mid, lo


def _ada_kernel(c_ref, w_ref, b_ref, o_ref):
    c = c_ref[...]
    s = _silu(c).astype(BF16)
    o_ref[0] = _dot(s, w_ref[0].astype(BF16)) + b_ref[0]


def _ada(c, ada_w, ada_b):
    r = c.shape[0]
    tn = 1536
    return pl.pallas_call(
        _ada_kernel,
        grid=(DEPTH, 6 * D_MODEL // tn),
        in_specs=[pl.BlockSpec((r, D_MODEL), lambda l, j: (0, 0)),
                  pl.BlockSpec((1, D_MODEL, tn), lambda l, j: (l, 0, j)),
                  pl.BlockSpec((1, 1, tn), lambda l, j: (l, 0, j))],
        out_specs=pl.BlockSpec((1, r, tn), lambda l, j: (l, 0, j)),
        out_shape=jax.ShapeDtypeStruct((DEPTH, r, 6 * D_MODEL), F32),
        compiler_params=_cparams(("arbitrary", "arbitrary")),
        name="ada",
    )(c, ada_w, ada_b.reshape(DEPTH, 1, 6 * D_MODEL))


def _qkv_kernel(x_ref, g_ref, sc_ref, sh_ref, w_ref, qkv_ref, k_ref, v_ref):
    h = _norm_mod(x_ref[0], g_ref[...], sc_ref[0], sh_ref[0]).astype(BF16)
    r = _dot(h, w_ref[...])
    qkv_ref[0, :, 0:D_QK] = (r[:, 0:D_QK] * QK_SCALE).astype(BF16)
    qkv_ref[0, :, D_QK:] = r[:, D_QK:].astype(BF16)
    k_ref[0] = r[:, D_QK:2 * D_QK]
    v_ref[0] = r[:, 2 * D_QK:]


def _qkv_proj(x, g, sc, sh, w):
    b, t, d = x.shape
    tm = min(t, 512)
    n = w.shape[1]
    row = lambda i, j: (i, j, 0)
    per_b = lambda i, j: (i, 0, 0)
    return pl.pallas_call(
        _qkv_kernel,
        grid=(b, t // tm),
        in_specs=[pl.BlockSpec((1, tm, d), row),
                  pl.BlockSpec((1, d), lambda i, j: (0, 0)),
                  pl.BlockSpec((1, 1, d), per_b),
                  pl.BlockSpec((1, 1, d), per_b),
                  pl.BlockSpec((d, n), lambda i, j: (0, 0))],
        out_specs=[pl.BlockSpec((1, tm, n), row),
                   pl.BlockSpec((1, tm, D_QK), row),
                   pl.BlockSpec((1, tm, D_ATT), row)],
        out_shape=[jax.ShapeDtypeStruct((b, t, n), BF16),
                   jax.ShapeDtypeStruct((b, t, D_QK), F32),
                   jax.ShapeDtypeStruct((b, t, D_ATT), F32)],
        compiler_params=_cparams(("arbitrary", "arbitrary")),
        name="qkv_proj",
    )(x, g, sc, sh, w)


def _attn_kernel(slopes_ref, lam_ref, subg_ref, q_ref, k_ref, v_ref, o_ref, m_sc, l_sc, acc_sc,
                 *, tq, tk, pos0, kv_len, lam_init, nk):
    hd = pl.program_id(1)
    qi = pl.program_id(2)
    ki = pl.program_id(3)
    slope = slopes_ref[hd]
    qs = pos0 + qi * tq
    ks = ki * tk
    last_chunk_end = ((qs + tq - 1) // CHUNK) * CHUNK + CHUNK - 1
    last_needed = jnp.minimum(nk - 1, last_chunk_end // tk)

    @pl.when(ki == 0)
    def _():
        m_sc[...] = jnp.full(m_sc.shape, NEG, F32)
        l_sc[...] = jnp.zeros(l_sc.shape, F32)
        acc_sc[...] = jnp.zeros(acc_sc.shape, F32)

    def update(bias, mask):
        q = q_ref[0]
        k = k_ref[0]
        v = v_ref[0]
        for m in range(2):
            s = _dot_nt(q[:, m * D_HEAD:(m + 1) * D_HEAD], k[:, m * D_HEAD:(m + 1) * D_HEAD]) + bias
            if mask is not None:
                s = jnp.where(mask, s, NEG)
            m_old = m_sc[m]
            m_new = jnp.maximum(m_old, jnp.max(s, axis=-1, keepdims=True))
            alpha = jnp.exp(m_old - m_new)
            p = jnp.exp(s - m_new)
            l_sc[m] = alpha * l_sc[m] + jnp.sum(p, axis=-1, keepdims=True)
            acc_sc[m] = alpha * acc_sc[m] + _dot(p.astype(BF16), v)
            m_sc[m] = m_new

    interior = jnp.logical_and(ks + tk - 1 <= qs, ks + tk <= kv_len)

    @pl.when(jnp.logical_and(ki <= last_needed, interior))
    def _():
        col = lax.broadcasted_iota(jnp.int32, (1, tk), 1) + (ks - qs)
        update(slope * col.astype(F32), None)

    @pl.when(jnp.logical_and(ki <= last_needed, jnp.logical_not(interior)))
    def _():
        row = lax.broadcasted_iota(jnp.int32, (tq, tk), 0)
        qpos = row + qs
        kpos = lax.broadcasted_iota(jnp.int32, (tq, tk), 1) + ks
        bias = slope * (row - jnp.abs(qpos - kpos)).astype(F32)
        mask = jnp.logical_and(jnp.right_shift(kpos, CHUNK_SHIFT) <= jnp.right_shift(qpos, CHUNK_SHIFT),
                               kpos < kv_len)
        update(bias, mask)

    @pl.when(ki == nk - 1)
    def _():
        lv = lam_ref[...]
        lam = (jnp.exp(jnp.sum(lv[0:1] * lv[1:2], axis=-1, keepdims=True))
               - jnp.exp(jnp.sum(lv[2:3] * lv[3:4], axis=-1, keepdims=True)) + lam_init)
        o = acc_sc[0] / l_sc[0] - lam * (acc_sc[1] / l_sc[1])
        o = _rms(o, subg_ref[...]) * (1.0 - lam_init)
        o_ref[0] = o.astype(o_ref.dtype)


def _attention(q_arr, q_col0, k_arr, k_col0, v_arr, v_col0, lam_vec, subln_g, lam_init,
               *, pos0, kv_len, tq, tk):
    b, t_q = q_arr.shape[0], q_arr.shape[1]
    t_k = k_arr.shape[1]
    nq, nk = t_q // tq, t_k // tk
    slopes = (2.0 ** (-8.0 * jnp.arange(1, H_ATT + 1, dtype=F32) / H_ATT)).astype(F32)

    def kv_block(qi, ki):
        last = ((pos0 + qi * tq + tq - 1) // CHUNK * CHUNK + CHUNK - 1) // tk
        return jnp.minimum(ki, jnp.minimum(last, nk - 1))

    kern = functools.partial(_attn_kernel, tq=tq, tk=tk, pos0=pos0, kv_len=kv_len,
                             lam_init=lam_init, nk=nk)
    return pl.pallas_call(
        kern,
        grid=(b, H_ATT, nq, nk),
        in_specs=[pl.BlockSpec(memory_space=pltpu.SMEM),
                  pl.BlockSpec((4, D_HEAD), lambda bb, h, qi, ki: (0, 0)),
                  pl.BlockSpec((1, D_V), lambda bb, h, qi, ki: (0, 0)),
                  pl.BlockSpec((1, tq, D_V), lambda bb, h, qi, ki: (bb, qi, q_col0 + h)),
                  pl.BlockSpec((1, tk, D_V), lambda bb, h, qi, ki: (bb, kv_block(qi, ki), k_col0 + h)),
                  pl.BlockSpec((1, tk, D_V), lambda bb, h, qi, ki: (bb, kv_block(qi, ki), v_col0 + h))],
        out_specs=pl.BlockSpec((1, tq, D_V), lambda bb, h, qi, ki: (bb, qi, h)),
        out_shape=jax.ShapeDtypeStruct((b, t_q, D_ATT), BF16),
        scratch_shapes=[pltpu.VMEM((2, tq, 1), F32), pltpu.VMEM((2, tq, 1), F32),
                        pltpu.VMEM((2, tq, D_V), F32)],
        compiler_params=_cparams(("arbitrary", "arbitrary", "arbitrary", "arbitrary")),
        name="diff_attn",
    )(slopes, lam_vec, subln_g.reshape(1, D_V), q_arr, k_arr, v_arr)


def _ssd_kernel(x_ref, g_ref, sc_ref, sh_ref, w_ref, convp_ref, ssmp_ref, cw_ref, cb_ref, dtb_ref,
                alog_ref, dskip_ref, ng_ref, tri_ref, e_ref, et_ref,
                y_ref, convo_ref, ssmo_ref, cbuf, state, ybuf, *, tl, n_valid):
    t = pl.program_id(1)

    @pl.when(t == 0)
    def _():
        cbuf[0:SUBLANES, :] = convp_ref[0]
        state[...] = ssmp_ref[0]

    h = _norm_mod(x_ref[0], g_ref[...], sc_ref[0], sh_ref[0]).astype(BF16)
    zxd = _dot(h, w_ref[...])
    z = zxd[:, 0:D_INNER]
    raw = zxd[:, D_INNER:D_INNER + D_XBC]
    dt_raw = zxd[:, D_INNER + D_XBC:]

    cbuf[SUBLANES:SUBLANES + tl, :] = raw
    conv = raw * cw_ref[SSM_CONV - 1:SSM_CONV, :] + cb_ref[...]
    for j in range(1, SSM_CONV):
        conv = conv + cbuf[SUBLANES - j:SUBLANES - j + tl, :] * cw_ref[SSM_CONV - 1 - j:SSM_CONV - j, :]
    convo_ref[0] = cbuf[SUBLANES + n_valid - (SSM_CONV - 1):SUBLANES + n_valid, :]
    cbuf[0:SUBLANES, :] = cbuf[tl:tl + SUBLANES, :]
    xbc = _silu(conv)
    xs = xbc[:, 0:D_INNER]
    bmat = xbc[:, D_INNER:D_INNER + N_GROUPS * D_STATE].astype(BF16)
    cmat = xbc[:, D_INNER + N_GROUPS * D_STATE:].astype(BF16)

    u = dt_raw + dtb_ref[...]
    dt = jnp.maximum(u, 0.0) + jnp.log1p(jnp.exp(-jnp.abs(u)))
    if n_valid < tl:
        rows = lax.broadcasted_iota(jnp.int32, dt.shape, 0)
        dt = jnp.where(rows < n_valid, dt, 0.0)
    da = dt * (-jnp.exp(alog_ref[...]))
    tri = tri_ref[...]
    acum = sum(_dot(tri, p) for p in _split3(da))
    acum_t = acum.T
    e = e_ref[...]
    dt_e = sum(_dot(p, e) for p in _split3(dt))
    acum_e = sum(_dot(p, e) for p in _split3(acum))
    xdt = xs * dt_e
    xdt_b = xdt.astype(BF16)
    last_e = acum_e[n_valid - 1:n_valid, :]
    xdec_b = (xdt * jnp.exp(last_e - acum_e)).astype(BF16)

    st = state[...]
    st_b = st.astype(BF16)
    causal = (lax.broadcasted_iota(jnp.int32, (tl, tl), 0) >= lax.broadcasted_iota(jnp.int32, (tl, tl), 1))
    lane = lax.broadcasted_iota(jnp.int32, (tl, LANES), 1)
    hpg = H_SSM // N_GROUPS
    gw = hpg * SSM_HEADDIM
    for g in range(N_GROUPS):
        bg = bmat[:, g * D_STATE:(g + 1) * D_STATE]
        cg = cmat[:, g * D_STATE:(g + 1) * D_STATE]
        cb = _dot_nt(cg, bg)
        ybuf[:, g * gw:(g + 1) * gw] = _dot_nt(cg, st_b[g * gw:(g + 1) * gw, :])
        for pr in range(hpg // 2):
            c0 = g * gw + pr * LANES
            x2 = xdt_b[:, c0:c0 + LANES]
            acc = None
            for half in range(2):
                hh = g * hpg + pr * 2 + half
                seg = acum[:, hh:hh + 1] - acum_t[hh:hh + 1, :]
                mm = (jnp.exp(jnp.where(causal, seg, NEG)) * cb).astype(BF16)
                keep = (lane < SSM_HEADDIM) if half == 0 else (lane >= SSM_HEADDIM)
                part = _dot(mm, jnp.where(keep, x2, jnp.zeros_like(x2)))
                acc = part if acc is None else acc + part
            ybuf[:, c0:c0 + LANES] = ybuf[:, c0:c0 + LANES] * jnp.exp(acum_e[:, c0:c0 + LANES]) + acc
        state[g * gw:(g + 1) * gw, :] = _dot_tn(xdec_b[:, g * gw:(g + 1) * gw], bg)

    f = jnp.exp(acum_t[:, n_valid - 1:n_valid])
    fb = jnp.broadcast_to(f, (LANES, D_STATE))
    fac = sum(_dot(et_ref[...], p) for p in _split3(fb))
    new_state = st * fac + state[...]
    state[...] = new_state
    ssmo_ref[0] = new_state

    y = ybuf[...] + dskip_ref[...] * xs
    y = y * _silu(z)
    gn = D_INNER // N_GROUPS
    for g in range(N_GROUPS):
        y_ref[0, :, g * gn:(g + 1) * gn] = _rms(y[:, g * gn:(g + 1) * gn],
                                                ng_ref[:, g * gn:(g + 1) * gn]).astype(y_ref.dtype)


def _ssd_branch(x, g, sc, sh, w_zxd, conv_past8, ssm_past, conv_w, conv_b, dt_bias, a_log, d_skip,
                norm_g, *, tl, n_valid):
    b, t, d = x.shape
    nt = t // tl
    assert n_valid == tl or nt == 1
    hp = H_SSM * SSM_HEADDIM
    tri = jnp.asarray(np.tril(np.ones((tl, tl), np.float32)), BF16)
    e_np = np.zeros((LANES, D_INNER), np.float32)
    for hh in range(H_SSM):
        e_np[hh, hh * SSM_HEADDIM:(hh + 1) * SSM_HEADDIM] = 1.0
    e = jnp.asarray(e_np, BF16)
    et = jnp.asarray(e_np.T, BF16)
    pad = lambda v: jnp.pad(v.astype(F32), (0, LANES - H_SSM)).reshape(1, LANES)
    row = lambda i, j: (i, j, 0)
    per_b = lambda i, j: (i, 0, 0)
    const2 = lambda i, j: (0, 0)
    kern = functools.partial(_ssd_kernel, tl=tl, n_valid=n_valid)
    return pl.pallas_call(
        kern,
        grid=(b, nt),
        in_specs=[pl.BlockSpec((1, tl, d), row),
                  pl.BlockSpec((1, d), const2),
                  pl.BlockSpec((1, 1, d), per_b),
                  pl.BlockSpec((1, 1, d), per_b),
                  pl.BlockSpec((d, D_ZXD), const2),
                  pl.BlockSpec((1, SUBLANES, D_XBC), per_b),
                  pl.BlockSpec((1, hp, D_STATE), per_b),
                  pl.BlockSpec((SSM_CONV, D_XBC), const2),
                  pl.BlockSpec((1, D_XBC), const2),
                  pl.BlockSpec((1, LANES), const2),
                  pl.BlockSpec((1, LANES), const2),
                  pl.BlockSpec((1, D_INNER), const2),
                  pl.BlockSpec((1, D_INNER), const2),
                  pl.BlockSpec((tl, tl), const2),
                  pl.BlockSpec((LANES, D_INNER), const2),
                  pl.BlockSpec((D_INNER, LANES), const2)],
        out_specs=[pl.BlockSpec((1, tl, D_INNER), row),
                   pl.BlockSpec((1, SSM_CONV - 1, D_XBC), per_b),
                   pl.BlockSpec((1, hp, D_STATE), per_b)],
        out_shape=[jax.ShapeDtypeStruct((b, t, D_INNER), BF16),
                   jax.ShapeDtypeStruct((b, SSM_CONV - 1, D_XBC), F32),
                   jax.ShapeDtypeStruct((b, hp, D_STATE), F32)],
        scratch_shapes=[pltpu.VMEM((SUBLANES + tl, D_XBC), F32),
                        pltpu.VMEM((hp, D_STATE), F32),
                        pltpu.VMEM((tl, D_INNER), F32)],
        compiler_params=_cparams(("arbitrary", "arbitrary")),
        name="ssd_branch",
    )(x, g, sc, sh, w_zxd, conv_past8, ssm_past, conv_w, conv_b.reshape(1, D_XBC), pad(dt_bias),
      pad(a_log), jnp.repeat(d_skip.astype(F32), SSM_HEADDIM).reshape(1, D_INNER),
      norm_g.reshape(1, D_INNER), tri, e, et)


def _mixout_kernel(o_ref, y_ref, x_ref, gate_ref, g1_ref, w_ref, out_ref):
    r = _dot(o_ref[0], w_ref[0:D_ATT, :]) + _dot(y_ref[0], w_ref[D_ATT:, :])
    out_ref[0] = x_ref[0] + gate_ref[0] * _rms(r, g1_ref[...])


def _mix_out(o, y, x, gate, g1, w):
    b, t, d = x.shape
    tm = min(t, 512)
    row = lambda i, j: (i, j, 0)
    per_b = lambda i, j: (i, 0, 0)
    return pl.pallas_call(
        _mixout_kernel,
        grid=(b, t // tm),
        in_specs=[pl.BlockSpec((1, tm, D_ATT), row),
                  pl.BlockSpec((1, tm, D_INNER), row),
                  pl.BlockSpec((1, tm, d), row),
                  pl.BlockSpec((1, 1, d), per_b),
                  pl.BlockSpec((1, d), lambda i, j: (0, 0)),
                  pl.BlockSpec((D_ATT + D_INNER, d), lambda i, j: (0, 0))],
        out_specs=pl.BlockSpec((1, tm, d), row),
        out_shape=jax.ShapeDtypeStruct((b, t, d), F32),
        compiler_params=_cparams(("arbitrary", "arbitrary")),
        name="mix_out",
    )(o, y, x, gate, g1, w)


def _conf_kernel(x_ref, g_ref, sc_ref, sh_ref, gate_ref, g1_ref, win_ref, bin_ref, past_ref, dww_ref,
                 dwb_ref, lng_ref, lnb_ref, wout_ref, bout_ref, out_ref, convo_ref, sb, act,
                 *, tt, rc):
    t = pl.program_id(1)

    @pl.when(t == 0)
    def _():
        sb[0, 0:CONF_HIST, :] = past_ref[0]

    h = _norm_mod(x_ref[0], g_ref[...], sc_ref[0], sh_ref[0]).astype(BF16)
    lin = _dot(h, win_ref[...]) + bin_ref[...]
    u = lin[:, 0:D_MODEL] * jax.nn.sigmoid(lin[:, D_MODEL:])
    sb[0, CONF_HIST:CONF_HIST + tt, :] = u
    span = tt + CONF_HIST - SUBLANES
    for r in range(1, SUBLANES):
        sb[r, 0:span, :] = sb[0, r:r + span, :]
    convo_ref[0] = sb[0, tt + CONF_HIST - (CONF_KW - 1):tt + CONF_HIST, :]

    off0 = CONF_HIST - (CONF_KW - 1)

    def chunk(i, carry):
        r0 = pl.multiple_of(i * rc, rc)
        acc = jnp.broadcast_to(dwb_ref[...], (rc, D_MODEL))
        for k in range(CONF_KW):
            a, r = divmod(k + off0, SUBLANES)
            acc = acc + sb[r, pl.ds(r0 + a * SUBLANES, rc), :] * dww_ref[k:k + 1, :]
        mu = jnp.mean(acc, axis=-1, keepdims=True)
        cen = acc - mu
        var = jnp.mean(cen * cen, axis=-1, keepdims=True)
        yn = cen * lax.rsqrt(var + EPS) * lng_ref[...] + lnb_ref[...]
        act[pl.ds(r0, rc), :] = _silu(yn).astype(BF16)
        return carry

    lax.fori_loop(0, tt // rc, chunk, 0)
    sb[0, 0:CONF_HIST, :] = sb[0, tt:tt + CONF_HIST, :]
    r = _dot(act[...], wout_ref[...]) + bout_ref[...]
    out_ref[0] = x_ref[0] + gate_ref[0] * _rms(r, g1_ref[...])


def _conformer(x, g, sc, sh, gate, g1, w_in, b_in, past32, dw_w, dw_b, ln_g, ln_b, w_out, b_out):
    b, t, d = x.shape
    tt = min(t, 256)
    rc = 16
    row = lambda i, j: (i, j, 0)
    per_b = lambda i, j: (i, 0, 0)
    const2 = lambda i, j: (0, 0)
    vec = lambda v: v.reshape(1, -1)
    kern = functools.partial(_conf_kernel, tt=tt, rc=rc)
    return pl.pallas_call(
        kern,
        grid=(b, t // tt),
        in_specs=[pl.BlockSpec((1, tt, d), row),
                  pl.BlockSpec((1, d), const2),
                  pl.BlockSpec((1, 1, d), per_b),
                  pl.BlockSpec((1, 1, d), per_b),
                  pl.BlockSpec((1, 1, d), per_b),
                  pl.BlockSpec((1, d), const2),
                  pl.BlockSpec((d, 2 * d), const2),
                  pl.BlockSpec((1, 2 * d), const2),
                  pl.BlockSpec((1, CONF_HIST, d), per_b),
                  pl.BlockSpec((CONF_KW, d), const2),
                  pl.BlockSpec((1, d), const2),
                  pl.BlockSpec((1, d), const2),
                  pl.BlockSpec((1, d), const2),
                  pl.BlockSpec((d, d), const2),
                  pl.BlockSpec((1, d), const2)],
        out_specs=[pl.BlockSpec((1, tt, d), row),
                   pl.BlockSpec((1, CONF_KW - 1, d), per_b)],
        out_shape=[jax.ShapeDtypeStruct((b, t, d), F32),
                   jax.ShapeDtypeStruct((b, CONF_KW - 1, d), F32)],
        scratch_shapes=[pltpu.VMEM((SUBLANES, CONF_HIST + tt, d), F32),
                        pltpu.VMEM((tt, d), BF16)],
        compiler_params=_cparams(("arbitrary", "arbitrary")),
        name="conformer",
    )(x, g, sc, sh, gate, g1, w_in, vec(b_in), past32, dw_w, vec(dw_b), vec(ln_g), vec(ln_b), w_out,
      vec(b_out))


def _ffn_kernel(x_ref, g2_ref, sc_ref, sh_ref, gate_ref, g3_ref, wg_ref, wu_ref, wd_ref, out_ref,
                h_sc, acc_sc, *, nf):
    f = pl.program_id(2)

    @pl.when(f == 0)
    def _():
        h_sc[...] = _norm_mod(x_ref[0], g2_ref[...], sc_ref[0], sh_ref[0]).astype(BF16)
        acc_sc[...] = jnp.zeros(acc_sc.shape, F32)

    h = h_sc[...]
    a = (_silu(_dot(h, wg_ref[...])) * _dot(h, wu_ref[...])).astype(BF16)
    acc_sc[...] += _dot(a, wd_ref[...])

    @pl.when(f == nf - 1)
    def _():
        out_ref[0] = x_ref[0] + gate_ref[0] * _rms(acc_sc[...], g3_ref[...])


def _ffn(x, g2, sc, sh, gate, g3, w_up, w_down):
    b, t, d = x.shape
    tm = min(t, 512)
    tf = D_FF // 2
    nf = D_FF // tf
    row = lambda i, j, f: (i, j, 0)
    per_b = lambda i, j, f: (i, 0, 0)
    const2 = lambda i, j, f: (0, 0)
    kern = functools.partial(_ffn_kernel, nf=nf)
    return pl.pallas_call(
        kern,
        grid=(b, t // tm, nf),
        in_specs=[pl.BlockSpec((1, tm, d), row),
                  pl.BlockSpec((1, d), const2),
                  pl.BlockSpec((1, 1, d), per_b),
                  pl.BlockSpec((1, 1, d), per_b),
                  pl.BlockSpec((1, 1, d), per_b),
                  pl.BlockSpec((1, d), const2),
                  pl.BlockSpec((d, tf), lambda i, j, f: (0, f)),
                  pl.BlockSpec((d, tf), lambda i, j, f: (0, f + nf)),
                  pl.BlockSpec((tf, d), lambda i, j, f: (f, 0))],
        out_specs=pl.BlockSpec((1, tm, d), row),
        out_shape=jax.ShapeDtypeStruct((b, t, d), F32),
        scratch_shapes=[pltpu.VMEM((tm, d), BF16), pltpu.VMEM((tm, d), F32)],
        compiler_params=_cparams(("arbitrary", "arbitrary", "arbitrary")),
        name="ffn",
    )(x, g2, sc, sh, gate, g3, w_up, w_up, w_down)


def _prep_weights(w):
    hyb = w["hyb_w_in"]
    dt_cols = hyb[:, :, 3 * D_QK + D_INNER + D_XBC:]
    zxd = jnp.concatenate(
        [hyb[:, :, 3 * D_QK:3 * D_QK + D_INNER + D_XBC],
         jnp.pad(dt_cols, ((0, 0), (0, 0), (0, D_DT_PAD - H_SSM)))], axis=-1)
    return dict(
        w_qkv=hyb[:, :, 0:3 * D_QK].astype(BF16),
        w_zxd=zxd.astype(BF16),
        hyb_w_out=w["hyb_w_out"].astype(BF16),
        ffn_w_up=w["ffn_w_up"].astype(BF16),
        ffn_w_down=w["ffn_w_down"].astype(BF16),
        conf_w_in=w["conf_w_in"].astype(BF16),
        conf_w_out=w["conf_w_out"].astype(BF16),
    )


def _trunk(x, mod, past, w, wb):
    b, t, d = x.shape
    ks, vs, sconvs, ssms, cconvs = [], [], [], [], []
    for l in range(DEPTH):
        pieces = [mod[l][:, None, i * d:(i + 1) * d] for i in range(6)]
        shift_m, scale_m, gate_m, shift_f, scale_f, gate_f = pieces
        ng = lambda i: w["norm_g"][l, i].reshape(1, d)
        j = l // 2
        if l % 2 == 0:
            lam_init = 0.8 - 0.6 * math.exp(-0.3 * l)
            qkv, k_rows, v_rows = _qkv_proj(x, ng(0), scale_m, shift_m, wb["w_qkv"][j])
            if past is None:
                tq = min(t, 512)
                o = _attention(qkv, 0, qkv, H_ATT, qkv, 2 * H_ATT, w["attn_lambda"][j],
                               w["attn_subln_g"][j], lam_init, pos0=0, kv_len=t, tq=tq, tk=tq)
                conv_past = jnp.zeros((b, SSM_CONV - 1, D_XBC), F32)
                ssm_past = jnp.zeros((b, H_SSM * SSM_HEADDIM, D_STATE), F32)
            else:
                pos0 = past[0].shape[2]
                kv_len = pos0 + t
                kv_pad = -(-kv_len // LANES) * LANES
                zpad = jnp.zeros((b, kv_pad - kv_len, D_QK), BF16)
                k_all = jnp.concatenate([past[0][j].reshape(b, pos0, D_QK).astype(BF16),
                                         qkv[:, :, D_QK:2 * D_QK], zpad], axis=1)
                v_all = jnp.concatenate([past[1][j].reshape(b, pos0, D_ATT).astype(BF16),
                                         qkv[:, :, 2 * D_QK:], zpad], axis=1)
                o = _attention(qkv, 0, k_all, 0, v_all, 0, w["attn_lambda"][j],
                               w["attn_subln_g"][j], lam_init, pos0=pos0, kv_len=kv_len, tq=t,
                               tk=kv_pad)
                conv_past = past[2][j]
                ssm_past = past[3][j].reshape(b, H_SSM * SSM_HEADDIM, D_STATE)
            conv_past8 = jnp.pad(conv_past, ((0, 0), (SUBLANES - (SSM_CONV - 1), 0), (0, 0)))
            tl = 256 if t % 256 == 0 else LANES
            t_pad = -(-t // tl) * tl
            x_ssd = x if t_pad == t else jnp.pad(x, ((0, 0), (0, t_pad - t), (0, 0)))
            y, conv_new, ssm_new = _ssd_branch(
                x_ssd, ng(0), scale_m, shift_m, wb["w_zxd"][j], conv_past8, ssm_past,
                w["ssm_conv_w"][j], w["ssm_conv_b"][j], w["ssm_dt_bias"][j], w["ssm_a_log"][j],
                w["ssm_d"][j], w["ssm_norm_g"][j], tl=tl, n_valid=tl if t_pad == t else t)
            y = y[:, :t]
            x = _mix_out(o, y, x, gate_m, ng(1), wb["hyb_w_out"][j])
            ks.append(k_rows.reshape(b, t, H_ATT, 2 * D_HEAD))
            vs.append(v_rows.reshape(b, t, H_ATT, D_V))
            sconvs.append(conv_new)
            ssms.append(ssm_new.reshape(b, H_SSM, SSM_HEADDIM, D_STATE))
        else:
            if past is None:
                past32 = jnp.zeros((b, CONF_HIST, d), F32)
            else:
                past32 = jnp.pad(past[4][j], ((0, 0), (CONF_HIST - (CONF_KW - 1), 0), (0, 0)))
            x, conv_new = _conformer(x, ng(0), scale_m, shift_m, gate_m, ng(1), wb["conf_w_in"][j],
                                     w["conf_b_in"][j], past32, w["conf_dw_w"][j], w["conf_dw_b"][j],
                                     w["conf_ln_g"][j], w["conf_ln_b"][j], wb["conf_w_out"][j],
                                     w["conf_b_out"][j])
            cconvs.append(conv_new)
        x = _ffn(x, ng(2), scale_f, shift_f, gate_f, ng(3), wb["ffn_w_up"][l], wb["ffn_w_down"][l])
    return x, jnp.stack(ks), jnp.stack(vs), jnp.stack(sconvs), jnp.stack(ssms), jnp.stack(cconvs)


def kernel(x_prompt, x_sample, c_prompt, c_sample, cache_attn_k, cache_attn_v, state_ssm_conv, state_ssm, state_conf_conv, ada_w, ada_b, norm_g, ffn_w_up, ffn_w_down, hyb_w_in, attn_lambda, attn_subln_g, ssm_conv_w, ssm_conv_b, ssm_dt_bias, ssm_a_log, ssm_d, ssm_norm_g, hyb_w_out, conf_w_in, conf_b_in, conf_dw_w, conf_dw_b, conf_ln_g, conf_ln_b, conf_w_out, conf_b_out):
    w = dict(norm_g=norm_g, ffn_w_up=ffn_w_up, ffn_w_down=ffn_w_down, hyb_w_in=hyb_w_in,
             attn_lambda=attn_lambda, attn_subln_g=attn_subln_g, ssm_conv_w=ssm_conv_w,
             ssm_conv_b=ssm_conv_b, ssm_dt_bias=ssm_dt_bias, ssm_a_log=ssm_a_log, ssm_d=ssm_d,
             ssm_norm_g=ssm_norm_g, hyb_w_out=hyb_w_out, conf_w_in=conf_w_in, conf_b_in=conf_b_in,
             conf_dw_w=conf_dw_w, conf_dw_b=conf_dw_b, conf_ln_g=conf_ln_g, conf_ln_b=conf_ln_b,
             conf_w_out=conf_w_out, conf_b_out=conf_b_out)
    wb = _prep_weights(w)
    bp, bs = c_prompt.shape[0], c_sample.shape[0]
    rows = -(-(bp + bs) // 16) * 16
    c_all = jnp.concatenate([c_prompt, c_sample, jnp.zeros((rows - bp - bs, D_MODEL), F32)], axis=0)
    mod = _ada(c_all, ada_w, ada_b)
    y_p, k_p, v_p, sconv_p, ssm_p, cconv_p = _trunk(x_prompt, mod[:, 0:bp], None, w, wb)
    past = (cache_attn_k, cache_attn_v, state_ssm_conv, state_ssm, state_conf_conv)
    y_s, k_s, v_s, sconv_s, ssm_s, cconv_s = _trunk(x_sample, mod[:, bp:bp + bs], past, w, wb)
    return (y_p, y_s, k_p, v_p, sconv_p, ssm_p, cconv_p, k_s, v_s, sconv_s, ssm_s, cconv_s)
```

```python
import functools
import math

import numpy as np
import jax
import jax.numpy as jnp
from jax import lax
from jax.experimental import pallas as pl
from jax.experimental.pallas import tpu as pltpu

F32 = jnp.float32
BF16 = jnp.bfloat16

D_MODEL = 1024
DEPTH = 4
CHUNK = 64
CHUNK_SHIFT = CHUNK.bit_length() - 1
assert 1 << CHUNK_SHIFT == CHUNK
EPS = 1e-6
H_ATT = 8
D_HEAD = D_MODEL // H_ATT // 2
D_V = 2 * D_HEAD
D_QK = H_ATT * 2 * D_HEAD
D_ATT = H_ATT * D_V
D_INNER = D_MODEL
SSM_HEADDIM = 64
H_SSM = D_INNER // SSM_HEADDIM
D_STATE = 128
N_GROUPS = 2
SSM_CONV = 4
D_XBC = D_INNER + 2 * N_GROUPS * D_STATE
CONF_KW = 31
D_FF = -(-8 * D_MODEL // (3 * 256)) * 256

LANES = 128
SUBLANES = 8
VMEM_LIMIT = 56 * 1024 * 1024
NEG = -1e30
LOG2E = math.log2(math.e)
QK_SCALE = D_HEAD ** -0.5 * LOG2E
D_DT_PAD = LANES
D_ZXD = D_INNER + D_XBC + D_DT_PAD
CONF_HIST = 32


def _cparams(sem):
    return pltpu.CompilerParams(dimension_semantics=sem, vmem_limit_bytes=VMEM_LIMIT)


def _dot(a, b):
    return jnp.dot(a, b, preferred_element_type=F32)


def _dot_nt(a, b):
    return lax.dot_general(a, b, (((1,), (1,)), ((), ())), preferred_element_type=F32)


def _dot_tn(a, b):
    return lax.dot_general(a, b, (((0,), (0,)), ((), ())), preferred_element_type=F32)


def _rms(xf, g):
    return xf * lax.rsqrt(jnp.mean(xf * xf, axis=-1, keepdims=True) + EPS) * g


def _norm_mod(xf, g, sc, sh):
    return _rms(xf, g) * (1.0 + sc) + sh


def _silu(x):
    return x * jax.nn.sigmoid(x)


def _split3(v):
    hi = v.astype(BF16)
    r = v - hi.astype(F32)
    mid = r.astype(BF16)
    lo = (r - mid.astype(F32)).astype(BF16)
    return hi, mid, lo


def _ada_kernel(c_ref, w_ref, b_ref, o_ref):
    c = c_ref[...]
    s = _silu(c).astype(BF16)
    o_ref[0] = _dot(s, w_ref[0].astype(BF16)) + b_ref[0]


def _ada(c, ada_w, ada_b):
    r = c.shape[0]
    tn = 1536
    return pl.pallas_call(
        _ada_kernel,
        grid=(DEPTH, 6 * D_MODEL // tn),
        in_specs=[pl.BlockSpec((r, D_MODEL), lambda l, j: (0, 0)),
                  pl.BlockSpec((1, D_MODEL, tn), lambda l, j: (l, 0, j)),
                  pl.BlockSpec((1, 1, tn), lambda l, j: (l, 0, j))],
        out_specs=pl.BlockSpec((1, r, tn), lambda l, j: (l, 0, j)),
        out_shape=jax.ShapeDtypeStruct((DEPTH, r, 6 * D_MODEL), F32),
        compiler_params=_cparams(("arbitrary", "arbitrary")),
        name="ada",
    )(c, ada_w, ada_b.reshape(DEPTH, 1, 6 * D_MODEL))


def _qkv_kernel(x_ref, g_ref, sc_ref, sh_ref, w_ref, qkv_ref, k_ref, v_ref):
    h = _norm_mod(x_ref[0], g_ref[...], sc_ref[0], sh_ref[0]).astype(BF16)
    r = _dot(h, w_ref[...])
    qkv_ref[0, :, 0:D_QK] = (r[:, 0:D_QK] * QK_SCALE).astype(BF16)
    qkv_ref[0, :, D_QK:] = r[:, D_QK:].astype(BF16)
    k_ref[0] = r[:, D_QK:2 * D_QK]
    v_ref[0] = r[:, 2 * D_QK:]


def _qkv_proj(x, g, sc, sh, w):
    b, t, d = x.shape
    tm = min(t, 512)
    n = w.shape[1]
    row = lambda i, j: (i, j, 0)
    per_b = lambda i, j: (i, 0, 0)
    return pl.pallas_call(
        _qkv_kernel,
        grid=(b, t // tm),
        in_specs=[pl.BlockSpec((1, tm, d), row),
                  pl.BlockSpec((1, d), lambda i, j: (0, 0)),
                  pl.BlockSpec((1, 1, d), per_b),
                  pl.BlockSpec((1, 1, d), per_b),
                  pl.BlockSpec((d, n), lambda i, j: (0, 0))],
        out_specs=[pl.BlockSpec((1, tm, n), row),
                   pl.BlockSpec((1, tm, D_QK), row),
                   pl.BlockSpec((1, tm, D_ATT), row)],
        out_shape=[jax.ShapeDtypeStruct((b, t, n), BF16),
                   jax.ShapeDtypeStruct((b, t, D_QK), F32),
                   jax.ShapeDtypeStruct((b, t, D_ATT), F32)],
        compiler_params=_cparams(("arbitrary", "arbitrary")),
        name="qkv_proj",
    )(x, g, sc, sh, w)


def _attn_kernel(slopes_ref, lam_ref, subg_ref, q_ref, k_ref, v_ref, o_ref, m_sc, l_sc, acc_sc,
                 *, tq, tk, pos0, kv_len, lam_init, nk):
    hd = pl.program_id(1)
    qi = pl.program_id(2)
    ki = pl.program_id(3)
    slope = slopes_ref[hd]
    qs = pos0 + qi * tq
    ks = ki * tk
    last_chunk_end = ((qs + tq - 1) // CHUNK) * CHUNK + CHUNK - 1
    last_needed = jnp.minimum(nk - 1, last_chunk_end // tk)

    @pl.when(ki == 0)
    def _():
        m_sc[...] = jnp.full(m_sc.shape, NEG, F32)
        l_sc[...] = jnp.zeros(l_sc.shape, F32)
        acc_sc[...] = jnp.zeros(acc_sc.shape, F32)

    def update(bias, mask):
        q = q_ref[0]
        k = k_ref[0]
        v = v_ref[0]
        for m in range(2):
            s = (_dot_nt(q[:, m * D_HEAD:(m + 1) * D_HEAD], k[:, m * D_HEAD:(m + 1) * D_HEAD])
                 + LOG2E * bias)
            if mask is not None:
                s = jnp.where(mask, s, NEG)
            m_old = m_sc[m]
            m_new = jnp.maximum(m_old, jnp.max(s, axis=-1, keepdims=True))
            alpha = jnp.exp2(m_old - m_new)
            p = jnp.exp2(s - m_new)
            l_sc[m] = alpha * l_sc[m] + jnp.sum(p, axis=-1, keepdims=True)
            acc_sc[m] = alpha * acc_sc[m] + _dot(p.astype(BF16), v)
            m_sc[m] = m_new

    interior = jnp.logical_and(ks + tk - 1 <= qs, ks + tk <= kv_len)

    @pl.when(jnp.logical_and(ki <= last_needed, interior))
    def _():
        col = lax.broadcasted_iota(jnp.int32, (1, tk), 1) + (ks - qs)
        update(slope * col.astype(F32), None)

    @pl.when(jnp.logical_and(ki <= last_needed, jnp.logical_not(interior)))
    def _():
        row = lax.broadcasted_iota(jnp.int32, (tq, tk), 0)
        qpos = row + qs
        kpos = lax.broadcasted_iota(jnp.int32, (tq, tk), 1) + ks
        bias = slope * (row - jnp.abs(qpos - kpos)).astype(F32)
        mask = jnp.logical_and(jnp.right_shift(kpos, CHUNK_SHIFT) <= jnp.right_shift(qpos, CHUNK_SHIFT),
                               kpos < kv_len)
        update(bias, mask)

    @pl.when(ki == nk - 1)
    def _():
        lv = lam_ref[...]
        lam = (jnp.exp(jnp.sum(lv[0:1] * lv[1:2], axis=-1, keepdims=True))
               - jnp.exp(jnp.sum(lv[2:3] * lv[3:4], axis=-1, keepdims=True)) + lam_init)
        o = acc_sc[0] / l_sc[0] - lam * (acc_sc[1] / l_sc[1])
        o = _rms(o, subg_ref[...]) * (1.0 - lam_init)
        o_ref[0] = o.astype(o_ref.dtype)


def _attention(q_arr, q_col0, k_arr, k_col0, v_arr, v_col0, lam_vec, subln_g, lam_init,
               *, pos0, kv_len, tq, tk):
    b, t_q = q_arr.shape[0], q_arr.shape[1]
    t_k = k_arr.shape[1]
    nq, nk = t_q // tq, t_k // tk
    slopes = (2.0 ** (-8.0 * jnp.arange(1, H_ATT + 1, dtype=F32) / H_ATT)).astype(F32)

    def kv_block(qi, ki):
        last = ((pos0 + qi * tq + tq - 1) // CHUNK * CHUNK + CHUNK - 1) // tk
        return jnp.minimum(ki, jnp.minimum(last, nk - 1))

    kern = functools.partial(_attn_kernel, tq=tq, tk=tk, pos0=pos0, kv_len=kv_len,
                             lam_init=lam_init, nk=nk)
    return pl.pallas_call(
        kern,
        grid=(b, H_ATT, nq, nk),
        in_specs=[pl.BlockSpec(memory_space=pltpu.SMEM),
                  pl.BlockSpec((4, D_HEAD), lambda bb, h, qi, ki: (0, 0)),
                  pl.BlockSpec((1, D_V), lambda bb, h, qi, ki: (0, 0)),
                  pl.BlockSpec((1, tq, D_V), lambda bb, h, qi, ki: (bb, qi, q_col0 + h)),
                  pl.BlockSpec((1, tk, D_V), lambda bb, h, qi, ki: (bb, kv_block(qi, ki), k_col0 + h)),
                  pl.BlockSpec((1, tk, D_V), lambda bb, h, qi, ki: (bb, kv_block(qi, ki), v_col0 + h))],
        out_specs=pl.BlockSpec((1, tq, D_V), lambda bb, h, qi, ki: (bb, qi, h)),
        out_shape=jax.ShapeDtypeStruct((b, t_q, D_ATT), BF16),
        scratch_shapes=[pltpu.VMEM((2, tq, 1), F32), pltpu.VMEM((2, tq, 1), F32),
                        pltpu.VMEM((2, tq, D_V), F32)],
        compiler_params=_cparams(("arbitrary", "arbitrary", "arbitrary", "arbitrary")),
        name="diff_attn",
    )(slopes, lam_vec, subln_g.reshape(1, D_V), q_arr, k_arr, v_arr)


N_AUG = 3
AUG_W = 2 * N_AUG
V_AUG_W = 2 * D_V


def _aug_lanes(lane, half, first, second):
    base = D_HEAD if half == 0 else 0
    out = jnp.zeros(first[0].shape, F32)
    for i, a in enumerate(list(first) + list(second)):
        out = jnp.where(lane == base + i, a.astype(F32), out)
    return out


def _augment(lane, half, feat, first, second):
    return jnp.where(_feat_lanes(lane, half), feat.astype(F32),
                     _aug_lanes(lane, half, first, second)).astype(BF16)


def _feat_lanes(lane, half):
    return (lane < D_HEAD) if half == 0 else (lane >= D_HEAD)


def _attn_seq_kernel(slopes_ref, lam_ref, subg_ref, q_ref, k_ref, v_ref, o_ref,
                     ka_sc, va_sc, qa_sc, s_buf, m_sc, mslot_sc, alpha_sc, acc_sc,
                     *, tq, t, lam_init):
    hd = pl.program_id(1)
    qi = pl.program_id(2)
    slope2 = slopes_ref[hd] * LOG2E
    nblk = t // tq
    rows_b = 512

    @pl.when(qi == 0)
    def _():
        def build(c, carry):
            r0 = pl.multiple_of(c * rows_b, rows_b)
            lane = lax.broadcasted_iota(jnp.int32, (rows_b, LANES), 1)
            pos = (lax.broadcasted_iota(jnp.int32, (rows_b, LANES), 0) + r0).astype(F32)
            parts = _split3(slope2 * pos)
            ones = [jnp.ones((rows_b, LANES), BF16)] * N_AUG
            kk = k_ref[0, pl.ds(r0, rows_b), :]
            for half in range(2):
                ka_sc[half, pl.ds(r0, rows_b), :] = _augment(lane, half, kk, parts, ones)
            va_sc[pl.ds(r0, rows_b), 0:D_V] = v_ref[0, pl.ds(r0, rows_b), :]
            va_sc[pl.ds(r0, rows_b), D_V:] = jnp.where(lane == 0, 1.0, 0.0).astype(BF16)
            return carry
        lax.fori_loop(0, t // rows_b, build, 0)

    lane_q = lax.broadcasted_iota(jnp.int32, (tq, LANES), 1)
    qpos = (lax.broadcasted_iota(jnp.int32, (tq, LANES), 0) + qi * tq).astype(F32)
    q_parts = _split3(-slope2 * qpos)
    q_ones = [jnp.ones((tq, LANES), BF16)] * N_AUG
    qq = q_ref[0]
    for half in range(2):
        qa_sc[half] = _augment(lane_q, half, qq, q_ones, q_parts)
    m_sc[...] = jnp.full(m_sc.shape, NEG, F32)
    acc_sc[...] = jnp.zeros(acc_sc.shape, F32)

    def stage_a(j, slot, diag):
        k0 = pl.multiple_of(j * tq, tq)
        for half in range(2):
            s = _dot_nt(qa_sc[half], ka_sc[half, pl.ds(k0, tq), :])
            if diag:
                ri = lax.broadcasted_iota(jnp.int32, (tq, tq), 0)
                cj = lax.broadcasted_iota(jnp.int32, (tq, tq), 1)
                s = s - (2.0 * slope2) * jnp.maximum(cj - ri, 0).astype(F32)
                vis = jnp.right_shift(cj, CHUNK_SHIFT) <= jnp.right_shift(ri, CHUNK_SHIFT)
                s = jnp.where(vis, s, NEG)
            s_buf[slot, half] = s
            mx = s[:, 0:LANES]
            for c in range(1, tq // LANES):
                mx = jnp.maximum(mx, s[:, c * LANES:(c + 1) * LANES])
            m_old = m_sc[half]
            m_new = jnp.maximum(m_old, jnp.max(mx, axis=-1, keepdims=True))
            alpha_sc[slot, half] = jnp.exp2(m_old - m_new)
            mslot_sc[slot, half] = m_new
            m_sc[half] = m_new

    def stage_b(j, slot):
        k0 = pl.multiple_of(j * tq, tq)
        vblk = va_sc[pl.ds(k0, tq), :]
        for half in range(2):
            mn = mslot_sc[slot, half]
            p = jnp.concatenate(
                [jnp.exp2(s_buf[slot, half, :, c * LANES:(c + 1) * LANES] - mn).astype(BF16)
                 for c in range(tq // LANES)], axis=-1)
            al = alpha_sc[slot, half]
            acc_sc[half] = jnp.concatenate([al, al], axis=-1) * acc_sc[half] + _dot(p, vblk)

    @pl.when(qi == 0)
    def _():
        stage_a(0, 0, True)
        stage_b(0, 0)

    @pl.when(qi > 0)
    def _():
        stage_a(0, 0, False)

        def body(j, carry):
            stage_a(j + 1, (j + 1) % 2, False)
            stage_b(j, j % 2)
            return carry
        lax.fori_loop(0, qi - 1, body, 0)
        stage_a(qi, qi % 2, True)
        stage_b(qi - 1, (qi - 1) % 2)
        stage_b(qi, qi % 2)

    lv = lam_ref[...]
    lam = (jnp.exp(jnp.sum(lv[0:1] * lv[1:2], axis=-1, keepdims=True))
           - jnp.exp(jnp.sum(lv[2:3] * lv[3:4], axis=-1, keepdims=True)) + lam_init)
    o = (acc_sc[0, :, 0:D_V] / acc_sc[0, :, D_V:D_V + 1]
         - lam * (acc_sc[1, :, 0:D_V] / acc_sc[1, :, D_V:D_V + 1]))
    o_ref[0] = (_rms(o, subg_ref[...]) * (1.0 - lam_init)).astype(o_ref.dtype)


def _attention_seq(qkv, lam_vec, subln_g, lam_init, *, tq):
    b, t = qkv.shape[0], qkv.shape[1]
    assert t % tq == 0 and tq % CHUNK == 0 and t % 512 == 0
    slopes = (2.0 ** (-8.0 * jnp.arange(1, H_ATT + 1, dtype=F32) / H_ATT)).astype(F32)
    kern = functools.partial(_attn_seq_kernel, tq=tq, t=t, lam_init=lam_init)
    return pl.pallas_call(
        kern,
        grid=(b, H_ATT, t // tq),
        in_specs=[pl.BlockSpec(memory_space=pltpu.SMEM),
                  pl.BlockSpec((4, D_HEAD), lambda bb, h, qi: (0, 0)),
                  pl.BlockSpec((1, D_V), lambda bb, h, qi: (0, 0)),
                  pl.BlockSpec((1, tq, D_V), lambda bb, h, qi: (bb, qi, h)),
                  pl.BlockSpec((1, t, D_V), lambda bb, h, qi: (bb, 0, H_ATT + h)),
                  pl.BlockSpec((1, t, D_V), lambda bb, h, qi: (bb, 0, 2 * H_ATT + h))],
        out_specs=pl.BlockSpec((1, tq, D_V), lambda bb, h, qi: (bb, qi, h)),
        out_shape=jax.ShapeDtypeStruct((b, t, D_ATT), BF16),
        scratch_shapes=[pltpu.VMEM((2, t, LANES), BF16),
                        pltpu.VMEM((t, V_AUG_W), BF16),
                        pltpu.VMEM((2, tq, LANES), BF16),
                        pltpu.VMEM((2, 2, tq, tq), F32),
                        pltpu.VMEM((2, tq, LANES), F32),
                        pltpu.VMEM((2, 2, tq, LANES), F32),
                        pltpu.VMEM((2, 2, tq, LANES), F32),
                        pltpu.VMEM((2, tq, V_AUG_W), F32)],
        compiler_params=_cparams(("arbitrary", "arbitrary", "arbitrary")),
        name="diff_attn_seq",
    )(slopes, lam_vec, subln_g.reshape(1, D_V), qkv, qkv, qkv)


def _ssd_kernel(x_ref, g_ref, sc_ref, sh_ref, w_ref, convp_ref, ssmp_ref, cw_ref, cb_ref, dtb_ref,
                alog_ref, dskip_ref, ng_ref, tri_ref, e_ref, et_ref,
                y_ref, convo_ref, ssmo_ref, cbuf, state, ybuf, *, tl, n_valid):
    t = pl.program_id(1)

    @pl.when(t == 0)
    def _():
        cbuf[0:SUBLANES, :] = convp_ref[0]
        state[...] = ssmp_ref[0]

    h = _norm_mod(x_ref[0], g_ref[...], sc_ref[0], sh_ref[0]).astype(BF16)
    zxd = _dot(h, w_ref[...])
    z = zxd[:, 0:D_INNER]
    raw = zxd[:, D_INNER:D_INNER + D_XBC]
    dt_raw = zxd[:, D_INNER + D_XBC:]

    cbuf[SUBLANES:SUBLANES + tl, :] = raw
    conv = raw * cw_ref[SSM_CONV - 1:SSM_CONV, :] + cb_ref[...]
    for j in range(1, SSM_CONV):
        conv = conv + cbuf[SUBLANES - j:SUBLANES - j + tl, :] * cw_ref[SSM_CONV - 1 - j:SSM_CONV - j, :]
    convo_ref[0] = cbuf[SUBLANES + n_valid - (SSM_CONV - 1):SUBLANES + n_valid, :]
    cbuf[0:SUBLANES, :] = cbuf[tl:tl + SUBLANES, :]
    xbc = _silu(conv)
    xs = xbc[:, 0:D_INNER]
    bmat = xbc[:, D_INNER:D_INNER + N_GROUPS * D_STATE].astype(BF16)
    cmat = xbc[:, D_INNER + N_GROUPS * D_STATE:].astype(BF16)

    u = dt_raw + dtb_ref[...]
    dt = jnp.maximum(u, 0.0) + jnp.log1p(jnp.exp(-jnp.abs(u)))
    if n_valid < tl:
        rows = lax.broadcasted_iota(jnp.int32, dt.shape, 0)
        dt = jnp.where(rows < n_valid, dt, 0.0)
    da = dt * (-jnp.exp(alog_ref[...]))
    tri = tri_ref[...]
    acum = sum(_dot(tri, p) for p in _split3(da))
    acum_t = acum.T
    e = e_ref[...]
    dt_e = sum(_dot(p, e) for p in _split3(dt))
    acum_e = sum(_dot(p, e) for p in _split3(acum))
    xdt = xs * dt_e
    xdt_b = xdt.astype(BF16)
    last_e = acum_e[n_valid - 1:n_valid, :]
    xdec_b = (xdt * jnp.exp(last_e - acum_e)).astype(BF16)

    st = state[...]
    st_b = st.astype(BF16)
    causal = (lax.broadcasted_iota(jnp.int32, (tl, tl), 0) >= lax.broadcasted_iota(jnp.int32, (tl, tl), 1))
    lane = lax.broadcasted_iota(jnp.int32, (tl, LANES), 1)
    hpg = H_SSM // N_GROUPS
    gw = hpg * SSM_HEADDIM
    for g in range(N_GROUPS):
        bg = bmat[:, g * D_STATE:(g + 1) * D_STATE]
        cg = cmat[:, g * D_STATE:(g + 1) * D_STATE]
        cb = _dot_nt(cg, bg)
        ybuf[:, g * gw:(g + 1) * gw] = _dot_nt(cg, st_b[g * gw:(g + 1) * gw, :])
        for pr in range(hpg // 2):
            c0 = g * gw + pr * LANES
            x2 = xdt_b[:, c0:c0 + LANES]
            acc = None
            for half in range(2):
                hh = g * hpg + pr * 2 + half
                seg = acum[:, hh:hh + 1] - acum_t[hh:hh + 1, :]
                mm = (jnp.exp(jnp.where(causal, seg, NEG)) * cb).astype(BF16)
                keep = (lane < SSM_HEADDIM) if half == 0 else (lane >= SSM_HEADDIM)
                part = _dot(mm, jnp.where(keep, x2, jnp.zeros_like(x2)))
                acc = part if acc is None else acc + part
            ybuf[:, c0:c0 + LANES] = ybuf[:, c0:c0 + LANES] * jnp.exp(acum_e[:, c0:c0 + LANES]) + acc
        state[g * gw:(g + 1) * gw, :] = _dot_tn(xdec_b[:, g * gw:(g + 1) * gw], bg)

    f = jnp.exp(acum_t[:, n_valid - 1:n_valid])
    fb = jnp.broadcast_to(f, (LANES, D_STATE))
    fac = sum(_dot(et_ref[...], p) for p in _split3(fb))
    new_state = st * fac + state[...]
    state[...] = new_state
    ssmo_ref[0] = new_state

    y = ybuf[...] + dskip_ref[...] * xs
    y = y * _silu(z)
    gn = D_INNER // N_GROUPS
    for g in range(N_GROUPS):
        y_ref[0, :, g * gn:(g + 1) * gn] = _rms(y[:, g * gn:(g + 1) * gn],
                                                ng_ref[:, g * gn:(g + 1) * gn]).astype(y_ref.dtype)


def _ssd_branch(x, g, sc, sh, w_zxd, conv_past8, ssm_past, conv_w, conv_b, dt_bias, a_log, d_skip,
                norm_g, *, tl, n_valid):
    b, t, d = x.shape
    nt = t // tl
    assert n_valid == tl or nt == 1
    hp = H_SSM * SSM_HEADDIM
    tri = jnp.asarray(np.tril(np.ones((tl, tl), np.float32)), BF16)
    e_np = np.zeros((LANES, D_INNER), np.float32)
    for hh in range(H_SSM):
        e_np[hh, hh * SSM_HEADDIM:(hh + 1) * SSM_HEADDIM] = 1.0
    e = jnp.asarray(e_np, BF16)
    et = jnp.asarray(e_np.T, BF16)
    pad = lambda v: jnp.pad(v.astype(F32), (0, LANES - H_SSM)).reshape(1, LANES)
    row = lambda i, j: (i, j, 0)
    per_b = lambda i, j: (i, 0, 0)
    const2 = lambda i, j: (0, 0)
    kern = functools.partial(_ssd_kernel, tl=tl, n_valid=n_valid)
    return pl.pallas_call(
        kern,
        grid=(b, nt),
        in_specs=[pl.BlockSpec((1, tl, d), row),
                  pl.BlockSpec((1, d), const2),
                  pl.BlockSpec((1, 1, d), per_b),
                  pl.BlockSpec((1, 1, d), per_b),
                  pl.BlockSpec((d, D_ZXD), const2),
                  pl.BlockSpec((1, SUBLANES, D_XBC), per_b),
                  pl.BlockSpec((1, hp, D_STATE), per_b),
                  pl.BlockSpec((SSM_CONV, D_XBC), const2),
                  pl.BlockSpec((1, D_XBC), const2),
                  pl.BlockSpec((1, LANES), const2),
                  pl.BlockSpec((1, LANES), const2),
                  pl.BlockSpec((1, D_INNER), const2),
                  pl.BlockSpec((1, D_INNER), const2),
                  pl.BlockSpec((tl, tl), const2),
                  pl.BlockSpec((LANES, D_INNER), const2),
                  pl.BlockSpec((D_INNER, LANES), const2)],
        out_specs=[pl.BlockSpec((1, tl, D_INNER), row),
                   pl.BlockSpec((1, SSM_CONV - 1, D_XBC), per_b),
                   pl.BlockSpec((1, hp, D_STATE), per_b)],
        out_shape=[jax.ShapeDtypeStruct((b, t, D_INNER), BF16),
                   jax.ShapeDtypeStruct((b, SSM_CONV - 1, D_XBC), F32),
                   jax.ShapeDtypeStruct((b, hp, D_STATE), F32)],
        scratch_shapes=[pltpu.VMEM((SUBLANES + tl, D_XBC), F32),
                        pltpu.VMEM((hp, D_STATE), F32),
                        pltpu.VMEM((tl, D_INNER), F32)],
        compiler_params=_cparams(("arbitrary", "arbitrary")),
        name="ssd_branch",
    )(x, g, sc, sh, w_zxd, conv_past8, ssm_past, conv_w, conv_b.reshape(1, D_XBC), pad(dt_bias),
      pad(a_log), jnp.repeat(d_skip.astype(F32), SSM_HEADDIM).reshape(1, D_INNER),
      norm_g.reshape(1, D_INNER), tri, e, et)


def _mixout_kernel(o_ref, y_ref, x_ref, gate_ref, g1_ref, w_ref, out_ref):
    r = _dot(o_ref[0], w_ref[0:D_ATT, :]) + _dot(y_ref[0], w_ref[D_ATT:, :])
    out_ref[0] = x_ref[0] + gate_ref[0] * _rms(r, g1_ref[...])


def _mix_out(o, y, x, gate, g1, w):
    b, t, d = x.shape
    tm = min(t, 512)
    row = lambda i, j: (i, j, 0)
    per_b = lambda i, j: (i, 0, 0)
    return pl.pallas_call(
        _mixout_kernel,
        grid=(b, t // tm),
        in_specs=[pl.BlockSpec((1, tm, D_ATT), row),
                  pl.BlockSpec((1, tm, D_INNER), row),
                  pl.BlockSpec((1, tm, d), row),
                  pl.BlockSpec((1, 1, d), per_b),
                  pl.BlockSpec((1, d), lambda i, j: (0, 0)),
                  pl.BlockSpec((D_ATT + D_INNER, d), lambda i, j: (0, 0))],
        out_specs=pl.BlockSpec((1, tm, d), row),
        out_shape=jax.ShapeDtypeStruct((b, t, d), F32),
        compiler_params=_cparams(("arbitrary", "arbitrary")),
        name="mix_out",
    )(o, y, x, gate, g1, w)


def _conf_kernel(x_ref, g_ref, sc_ref, sh_ref, gate_ref, g1_ref, win_ref, bin_ref, past_ref, dww_ref,
                 dwb_ref, lng_ref, lnb_ref, wout_ref, bout_ref, out_ref, convo_ref, sb, act,
                 *, tt, rc):
    t = pl.program_id(1)

    @pl.when(t == 0)
    def _():
        sb[0, 0:CONF_HIST, :] = past_ref[0]

    h = _norm_mod(x_ref[0], g_ref[...], sc_ref[0], sh_ref[0]).astype(BF16)
    lin = _dot(h, win_ref[...]) + bin_ref[...]
    u = lin[:, 0:D_MODEL] * jax.nn.sigmoid(lin[:, D_MODEL:])
    sb[0, CONF_HIST:CONF_HIST + tt, :] = u
    span = tt + CONF_HIST - SUBLANES
    for r in range(1, SUBLANES):
        sb[r, 0:span, :] = sb[0, r:r + span, :]
    convo_ref[0] = sb[0, tt + CONF_HIST - (CONF_KW - 1):tt + CONF_HIST, :]

    off0 = CONF_HIST - (CONF_KW - 1)

    def chunk(i, carry):
        r0 = pl.multiple_of(i * rc, rc)
        acc = jnp.broadcast_to(dwb_ref[...], (rc, D_MODEL))
        for k in range(CONF_KW):
            a, r = divmod(k + off0, SUBLANES)
            acc = acc + sb[r, pl.ds(r0 + a * SUBLANES, rc), :] * dww_ref[k:k + 1, :]
        mu = jnp.mean(acc, axis=-1, keepdims=True)
        cen = acc - mu
        var = jnp.mean(cen * cen, axis=-1, keepdims=True)
        yn = cen * lax.rsqrt(var + EPS) * lng_ref[...] + lnb_ref[...]
        act[pl.ds(r0, rc), :] = _silu(yn).astype(BF16)
        return carry

    lax.fori_loop(0, tt // rc, chunk, 0)
    sb[0, 0:CONF_HIST, :] = sb[0, tt:tt + CONF_HIST, :]
    r = _dot(act[...], wout_ref[...]) + bout_ref[...]
    out_ref[0] = x_ref[0] + gate_ref[0] * _rms(r, g1_ref[...])


def _conformer(x, g, sc, sh, gate, g1, w_in, b_in, past32, dw_w, dw_b, ln_g, ln_b, w_out, b_out):
    b, t, d = x.shape
    tt = min(t, 256)
    rc = 16
    row = lambda i, j: (i, j, 0)
    per_b = lambda i, j: (i, 0, 0)
    const2 = lambda i, j: (0, 0)
    vec = lambda v: v.reshape(1, -1)
    kern = functools.partial(_conf_kernel, tt=tt, rc=rc)
    return pl.pallas_call(
        kern,
        grid=(b, t // tt),
        in_specs=[pl.BlockSpec((1, tt, d), row),
                  pl.BlockSpec((1, d), const2),
                  pl.BlockSpec((1, 1, d), per_b),
                  pl.BlockSpec((1, 1, d), per_b),
                  pl.BlockSpec((1, 1, d), per_b),
                  pl.BlockSpec((1, d), const2),
                  pl.BlockSpec((d, 2 * d), const2),
                  pl.BlockSpec((1, 2 * d), const2),
                  pl.BlockSpec((1, CONF_HIST, d), per_b),
                  pl.BlockSpec((CONF_KW, d), const2),
                  pl.BlockSpec((1, d), const2),
                  pl.BlockSpec((1, d), const2),
                  pl.BlockSpec((1, d), const2),
                  pl.BlockSpec((d, d), const2),
                  pl.BlockSpec((1, d), const2)],
        out_specs=[pl.BlockSpec((1, tt, d), row),
                   pl.BlockSpec((1, CONF_KW - 1, d), per_b)],
        out_shape=[jax.ShapeDtypeStruct((b, t, d), F32),
                   jax.ShapeDtypeStruct((b, CONF_KW - 1, d), F32)],
        scratch_shapes=[pltpu.VMEM((SUBLANES, CONF_HIST + tt, d), F32),
                        pltpu.VMEM((tt, d), BF16)],
        compiler_params=_cparams(("arbitrary", "arbitrary")),
        name="conformer",
    )(x, g, sc, sh, gate, g1, w_in, vec(b_in), past32, dw_w, vec(dw_b), vec(ln_g), vec(ln_b), w_out,
      vec(b_out))


def _ffn_kernel(x_ref, g2_ref, sc_ref, sh_ref, gate_ref, g3_ref, wg_ref, wu_ref, wd_ref, out_ref,
                h_sc, acc_sc, *, nf):
    f = pl.program_id(2)

    @pl.when(f == 0)
    def _():
        h_sc[...] = _norm_mod(x_ref[0], g2_ref[...], sc_ref[0], sh_ref[0]).astype(BF16)
        acc_sc[...] = jnp.zeros(acc_sc.shape, F32)

    h = h_sc[...]
    a = (_silu(_dot(h, wg_ref[...])) * _dot(h, wu_ref[...])).astype(BF16)
    acc_sc[...] += _dot(a, wd_ref[...])

    @pl.when(f == nf - 1)
    def _():
        out_ref[0] = x_ref[0] + gate_ref[0] * _rms(acc_sc[...], g3_ref[...])


def _ffn(x, g2, sc, sh, gate, g3, w_up, w_down):
    b, t, d = x.shape
    tm = min(t, 512)
    tf = D_FF // 2
    nf = D_FF // tf
    row = lambda i, j, f: (i, j, 0)
    per_b = lambda i, j, f: (i, 0, 0)
    const2 = lambda i, j, f: (0, 0)
    kern = functools.partial(_ffn_kernel, nf=nf)
    return pl.pallas_call(
        kern,
        grid=(b, t // tm, nf),
        in_specs=[pl.BlockSpec((1, tm, d), row),
                  pl.BlockSpec((1, d), const2),
                  pl.BlockSpec((1, 1, d), per_b),
                  pl.BlockSpec((1, 1, d), per_b),
                  pl.BlockSpec((1, 1, d), per_b),
                  pl.BlockSpec((1, d), const2),
                  pl.BlockSpec((d, tf), lambda i, j, f: (0, f)),
                  pl.BlockSpec((d, tf), lambda i, j, f: (0, f + nf)),
                  pl.BlockSpec((tf, d), lambda i, j, f: (f, 0))],
        out_specs=pl.BlockSpec((1, tm, d), row),
        out_shape=jax.ShapeDtypeStruct((b, t, d), F32),
        scratch_shapes=[pltpu.VMEM((tm, d), BF16), pltpu.VMEM((tm, d), F32)],
        compiler_params=_cparams(("arbitrary", "arbitrary", "arbitrary")),
        name="ffn",
    )(x, g2, sc, sh, gate, g3, w_up, w_up, w_down)


def _prep_weights(w):
    hyb = w["hyb_w_in"]
    dt_cols = hyb[:, :, 3 * D_QK + D_INNER + D_XBC:]
    zxd = jnp.concatenate(
        [hyb[:, :, 3 * D_QK:3 * D_QK + D_INNER + D_XBC],
         jnp.pad(dt_cols, ((0, 0), (0, 0), (0, D_DT_PAD - H_SSM)))], axis=-1)
    return dict(
        w_qkv=hyb[:, :, 0:3 * D_QK].astype(BF16),
        w_zxd=zxd.astype(BF16),
        hyb_w_out=w["hyb_w_out"].astype(BF16),
        ffn_w_up=w["ffn_w_up"].astype(BF16),
        ffn_w_down=w["ffn_w_down"].astype(BF16),
        conf_w_in=w["conf_w_in"].astype(BF16),
        conf_w_out=w["conf_w_out"].astype(BF16),
    )


def _trunk(x, mod, past, w, wb):
    b, t, d = x.shape
    ks, vs, sconvs, ssms, cconvs = [], [], [], [], []
    for l in range(DEPTH):
        pieces = [mod[l][:, None, i * d:(i + 1) * d] for i in range(6)]
        shift_m, scale_m, gate_m, shift_f, scale_f, gate_f = pieces
        ng = lambda i: w["norm_g"][l, i].reshape(1, d)
        j = l // 2
        if l % 2 == 0:
            lam_init = 0.8 - 0.6 * math.exp(-0.3 * l)
            qkv, k_rows, v_rows = _qkv_proj(x, ng(0), scale_m, shift_m, wb["w_qkv"][j])
            if past is None:
                o = _attention_seq(qkv, w["attn_lambda"][j], w["attn_subln_g"][j], lam_init,
                                   tq=min(t, 512))
                conv_past = jnp.zeros((b, SSM_CONV - 1, D_XBC), F32)
                ssm_past = jnp.zeros((b, H_SSM * SSM_HEADDIM, D_STATE), F32)
            else:
                pos0 = past[0].shape[2]
                kv_len = pos0 + t
                kv_pad = -(-kv_len // LANES) * LANES
                zpad = jnp.zeros((b, kv_pad - kv_len, D_QK), BF16)
                k_all = jnp.concatenate([past[0][j].reshape(b, pos0, D_QK).astype(BF16),
                                         qkv[:, :, D_QK:2 * D_QK], zpad], axis=1)
                v_all = jnp.concatenate([past[1][j].reshape(b, pos0, D_ATT).astype(BF16),
                                         qkv[:, :, 2 * D_QK:], zpad], axis=1)
                o = _attention(qkv, 0, k_all, 0, v_all, 0, w["attn_lambda"][j],
                               w["attn_subln_g"][j], lam_init, pos0=pos0, kv_len=kv_len, tq=t,
                               tk=kv_pad)
                conv_past = past[2][j]
                ssm_past = past[3][j].reshape(b, H_SSM * SSM_HEADDIM, D_STATE)
            conv_past8 = jnp.pad(conv_past, ((0, 0), (SUBLANES - (SSM_CONV - 1), 0), (0, 0)))
            tl = 256 if t % 256 == 0 else LANES
            t_pad = -(-t // tl) * tl
            x_ssd = x if t_pad == t else jnp.pad(x, ((0, 0), (0, t_pad - t), (0, 0)))
            y, conv_new, ssm_new = _ssd_branch(
                x_ssd, ng(0), scale_m, shift_m, wb["w_zxd"][j], conv_past8, ssm_past,
                w["ssm_conv_w"][j], w["ssm_conv_b"][j], w["ssm_dt_bias"][j], w["ssm_a_log"][j],
                w["ssm_d"][j], w["ssm_norm_g"][j], tl=tl, n_valid=tl if t_pad == t else t)
            y = y[:, :t]
            x = _mix_out(o, y, x, gate_m, ng(1), wb["hyb_w_out"][j])
            ks.append(k_rows.reshape(b, t, H_ATT, 2 * D_HEAD))
            vs.append(v_rows.reshape(b, t, H_ATT, D_V))
            sconvs.append(conv_new)
            ssms.append(ssm_new.reshape(b, H_SSM, SSM_HEADDIM, D_STATE))
        else:
            if past is None:
                past32 = jnp.zeros((b, CONF_HIST, d), F32)
            else:
                past32 = jnp.pad(past[4][j], ((0, 0), (CONF_HIST - (CONF_KW - 1), 0), (0, 0)))
            x, conv_new = _conformer(x, ng(0), scale_m, shift_m, gate_m, ng(1), wb["conf_w_in"][j],
                                     w["conf_b_in"][j], past32, w["conf_dw_w"][j], w["conf_dw_b"][j],
                                     w["conf_ln_g"][j], w["conf_ln_b"][j], wb["conf_w_out"][j],
                                     w["conf_b_out"][j])
            cconvs.append(conv_new)
        x = _ffn(x, ng(2), scale_f, shift_f, gate_f, ng(3), wb["ffn_w_up"][l], wb["ffn_w_down"][l])
    return x, jnp.stack(ks), jnp.stack(vs), jnp.stack(sconvs), jnp.stack(ssms), jnp.stack(cconvs)


def kernel(x_prompt, x_sample, c_prompt, c_sample, cache_attn_k, cache_attn_v, state_ssm_conv, state_ssm, state_conf_conv, ada_w, ada_b, norm_g, ffn_w_up, ffn_w_down, hyb_w_in, attn_lambda, attn_subln_g, ssm_conv_w, ssm_conv_b, ssm_dt_bias, ssm_a_log, ssm_d, ssm_norm_g, hyb_w_out, conf_w_in, conf_b_in, conf_dw_w, conf_dw_b, conf_ln_g, conf_ln_b, conf_w_out, conf_b_out):
    w = dict(norm_g=norm_g, ffn_w_up=ffn_w_up, ffn_w_down=ffn_w_down, hyb_w_in=hyb_w_in,
             attn_lambda=attn_lambda, attn_subln_g=attn_subln_g, ssm_conv_w=ssm_conv_w,
             ssm_conv_b=ssm_conv_b, ssm_dt_bias=ssm_dt_bias, ssm_a_log=ssm_a_log, ssm_d=ssm_d,
             ssm_norm_g=ssm_norm_g, hyb_w_out=hyb_w_out, conf_w_in=conf_w_in, conf_b_in=conf_b_in,
             conf_dw_w=conf_dw_w, conf_dw_b=conf_dw_b, conf_ln_g=conf_ln_g, conf_ln_b=conf_ln_b,
             conf_w_out=conf_w_out, conf_b_out=conf_b_out)
    wb = _prep_weights(w)
    bp, bs = c_prompt.shape[0], c_sample.shape[0]
    rows = -(-(bp + bs) // 16) * 16
    c_all = jnp.concatenate([c_prompt, c_sample, jnp.zeros((rows - bp - bs, D_MODEL), F32)], axis=0)
    mod = _ada(c_all, ada_w, ada_b)
    y_p, k_p, v_p, sconv_p, ssm_p, cconv_p = _trunk(x_prompt, mod[:, 0:bp], None, w, wb)
    past = (cache_attn_k, cache_attn_v, state_ssm_conv, state_ssm, state_conf_conv)
    y_s, k_s, v_s, sconv_s, ssm_s, cconv_s = _trunk(x_sample, mod[:, bp:bp + bs], past, w, wb)
    return (y_p, y_s, k_p, v_p, sconv_p, ssm_p, cconv_p, k_s, v_s, sconv_s, ssm_s, cconv_s)
```

```python
import functools
import math

import numpy as np
import jax
import jax.numpy as jnp
from jax import lax
from jax.experimental import pallas as pl
from jax.experimental.pallas import tpu as pltpu

F32 = jnp.float32
BF16 = jnp.bfloat16

D_MODEL = 1024
DEPTH = 4
CHUNK = 64
CHUNK_SHIFT = CHUNK.bit_length() - 1
assert 1 << CHUNK_SHIFT == CHUNK
EPS = 1e-6
H_ATT = 8
D_HEAD = D_MODEL // H_ATT // 2
D_V = 2 * D_HEAD
D_QK = H_ATT * 2 * D_HEAD
D_ATT = H_ATT * D_V
D_INNER = D_MODEL
SSM_HEADDIM = 64
H_SSM = D_INNER // SSM_HEADDIM
D_STATE = 128
N_GROUPS = 2
SSM_CONV = 4
D_XBC = D_INNER + 2 * N_GROUPS * D_STATE
CONF_KW = 31
D_FF = -(-8 * D_MODEL // (3 * 256)) * 256

LANES = 128
SUBLANES = 8
VMEM_LIMIT = 56 * 1024 * 1024
NEG = -1e30
LOG2E = math.log2(math.e)
QK_SCALE = D_HEAD ** -0.5 * LOG2E
D_DT_PAD = LANES
D_ZXD = D_INNER + D_XBC + D_DT_PAD
CONF_HIST = 32


def _cparams(sem):
    return pltpu.CompilerParams(dimension_semantics=sem, vmem_limit_bytes=VMEM_LIMIT)


def _dot(a, b):
    return jnp.dot(a, b, preferred_element_type=F32)


def _dot_nt(a, b):
    return lax.dot_general(a, b, (((1,), (1,)), ((), ())), preferred_element_type=F32)


def _dot_tn(a, b):
    return lax.dot_general(a, b, (((0,), (0,)), ((), ())), preferred_element_type=F32)


def _rms(xf, g):
    return xf * lax.rsqrt(jnp.mean(xf * xf, axis=-1, keepdims=True) + EPS) * g


def _norm_mod(xf, g, sc, sh):
    return _rms(xf, g) * (1.0 + sc) + sh


def _silu(x):
    return x * jax.nn.sigmoid(x)


def _split3(v):
    hi = v.astype(BF16)
    r = v - hi.astype(F32)
    mid = r.astype(BF16)
    lo = (r - mid.astype(F32)).astype(BF16)
    return hi, mid, lo


def _ada_kernel(c_ref, w_ref, b_ref, o_ref):
    c = c_ref[...]
    s = _silu(c).astype(BF16)
    o_ref[0] = _dot(s, w_ref[0].astype(BF16)) + b_ref[0]


def _ada(c, ada_w, ada_b):
    r = c.shape[0]
    tn = 1536
    return pl.pallas_call(
        _ada_kernel,
        grid=(DEPTH, 6 * D_MODEL // tn),
        in_specs=[pl.BlockSpec((r, D_MODEL), lambda l, j: (0, 0)),
                  pl.BlockSpec((1, D_MODEL, tn), lambda l, j: (l, 0, j)),
                  pl.BlockSpec((1, 1, tn), lambda l, j: (l, 0, j))],
        out_specs=pl.BlockSpec((1, r, tn), lambda l, j: (l, 0, j)),
        out_shape=jax.ShapeDtypeStruct((DEPTH, r, 6 * D_MODEL), F32),
        compiler_params=_cparams(("arbitrary", "arbitrary")),
        name="ada",
    )(c, ada_w, ada_b.reshape(DEPTH, 1, 6 * D_MODEL))


def _qkv_kernel(x_ref, g_ref, sc_ref, sh_ref, w_ref, qkv_ref, k_ref, v_ref):
    h = _norm_mod(x_ref[0], g_ref[...], sc_ref[0], sh_ref[0]).astype(BF16)
    r = _dot(h, w_ref[...])
    qkv_ref[0, :, 0:D_QK] = (r[:, 0:D_QK] * QK_SCALE).astype(BF16)
    qkv_ref[0, :, D_QK:] = r[:, D_QK:].astype(BF16)
    k_ref[0] = r[:, D_QK:2 * D_QK]
    v_ref[0] = r[:, 2 * D_QK:]


def _qkv_proj(x, g, sc, sh, w):
    b, t, d = x.shape
    tm = min(t, 512)
    n = w.shape[1]
    row = lambda i, j: (i, j, 0)
    per_b = lambda i, j: (i, 0, 0)
    return pl.pallas_call(
        _qkv_kernel,
        grid=(b, t // tm),
        in_specs=[pl.BlockSpec((1, tm, d), row),
                  pl.BlockSpec((1, d), lambda i, j: (0, 0)),
                  pl.BlockSpec((1, 1, d), per_b),
                  pl.BlockSpec((1, 1, d), per_b),
                  pl.BlockSpec((d, n), lambda i, j: (0, 0))],
        out_specs=[pl.BlockSpec((1, tm, n), row),
                   pl.BlockSpec((1, tm, D_QK), row),
                   pl.BlockSpec((1, tm, D_ATT), row)],
        out_shape=[jax.ShapeDtypeStruct((b, t, n), BF16),
                   jax.ShapeDtypeStruct((b, t, D_QK), F32),
                   jax.ShapeDtypeStruct((b, t, D_ATT), F32)],
        compiler_params=_cparams(("arbitrary", "arbitrary")),
        name="qkv_proj",
    )(x, g, sc, sh, w)


def _attn_kernel(slopes_ref, lam_ref, subg_ref, q_ref, k_ref, v_ref, o_ref, m_sc, l_sc, acc_sc,
                 *, tq, tk, pos0, kv_len, lam_init, nk):
    hd = pl.program_id(1)
    qi = pl.program_id(2)
    ki = pl.program_id(3)
    slope = slopes_ref[hd]
    qs = pos0 + qi * tq
    ks = ki * tk
    last_chunk_end = ((qs + tq - 1) // CHUNK) * CHUNK + CHUNK - 1
    last_needed = jnp.minimum(nk - 1, last_chunk_end // tk)

    @pl.when(ki == 0)
    def _():
        m_sc[...] = jnp.full(m_sc.shape, NEG, F32)
        l_sc[...] = jnp.zeros(l_sc.shape, F32)
        acc_sc[...] = jnp.zeros(acc_sc.shape, F32)

    def update(bias, mask):
        q = q_ref[0]
        k = k_ref[0]
        v = v_ref[0]
        for m in range(2):
            s = (_dot_nt(q[:, m * D_HEAD:(m + 1) * D_HEAD], k[:, m * D_HEAD:(m + 1) * D_HEAD])
                 + LOG2E * bias)
            if mask is not None:
                s = jnp.where(mask, s, NEG)
            m_old = m_sc[m]
            m_new = jnp.maximum(m_old, jnp.max(s, axis=-1, keepdims=True))
            alpha = jnp.exp2(m_old - m_new)
            p = jnp.exp2(s - m_new)
            l_sc[m] = alpha * l_sc[m] + jnp.sum(p, axis=-1, keepdims=True)
            acc_sc[m] = alpha * acc_sc[m] + _dot(p.astype(BF16), v)
            m_sc[m] = m_new

    interior = jnp.logical_and(ks + tk - 1 <= qs, ks + tk <= kv_len)

    @pl.when(jnp.logical_and(ki <= last_needed, interior))
    def _():
        col = lax.broadcasted_iota(jnp.int32, (1, tk), 1) + (ks - qs)
        update(slope * col.astype(F32), None)

    @pl.when(jnp.logical_and(ki <= last_needed, jnp.logical_not(interior)))
    def _():
        row = lax.broadcasted_iota(jnp.int32, (tq, tk), 0)
        qpos = row + qs
        kpos = lax.broadcasted_iota(jnp.int32, (tq, tk), 1) + ks
        bias = slope * (row - jnp.abs(qpos - kpos)).astype(F32)
        mask = jnp.logical_and(jnp.right_shift(kpos, CHUNK_SHIFT) <= jnp.right_shift(qpos, CHUNK_SHIFT),
                               kpos < kv_len)
        update(bias, mask)

    @pl.when(ki == nk - 1)
    def _():
        lv = lam_ref[...]
        lam = (jnp.exp(jnp.sum(lv[0:1] * lv[1:2], axis=-1, keepdims=True))
               - jnp.exp(jnp.sum(lv[2:3] * lv[3:4], axis=-1, keepdims=True)) + lam_init)
        o = acc_sc[0] / l_sc[0] - lam * (acc_sc[1] / l_sc[1])
        o = _rms(o, subg_ref[...]) * (1.0 - lam_init)
        o_ref[0] = o.astype(o_ref.dtype)


def _attention(q_arr, q_col0, k_arr, k_col0, v_arr, v_col0, lam_vec, subln_g, lam_init,
               *, pos0, kv_len, tq, tk):
    b, t_q = q_arr.shape[0], q_arr.shape[1]
    t_k = k_arr.shape[1]
    nq, nk = t_q // tq, t_k // tk
    slopes = (2.0 ** (-8.0 * jnp.arange(1, H_ATT + 1, dtype=F32) / H_ATT)).astype(F32)

    def kv_block(qi, ki):
        last = ((pos0 + qi * tq + tq - 1) // CHUNK * CHUNK + CHUNK - 1) // tk
        return jnp.minimum(ki, jnp.minimum(last, nk - 1))

    kern = functools.partial(_attn_kernel, tq=tq, tk=tk, pos0=pos0, kv_len=kv_len,
                             lam_init=lam_init, nk=nk)
    return pl.pallas_call(
        kern,
        grid=(b, H_ATT, nq, nk),
        in_specs=[pl.BlockSpec(memory_space=pltpu.SMEM),
                  pl.BlockSpec((4, D_HEAD), lambda bb, h, qi, ki: (0, 0)),
                  pl.BlockSpec((1, D_V), lambda bb, h, qi, ki: (0, 0)),
                  pl.BlockSpec((1, tq, D_V), lambda bb, h, qi, ki: (bb, qi, q_col0 + h)),
                  pl.BlockSpec((1, tk, D_V), lambda bb, h, qi, ki: (bb, kv_block(qi, ki), k_col0 + h)),
                  pl.BlockSpec((1, tk, D_V), lambda bb, h, qi, ki: (bb, kv_block(qi, ki), v_col0 + h))],
        out_specs=pl.BlockSpec((1, tq, D_V), lambda bb, h, qi, ki: (bb, qi, h)),
        out_shape=jax.ShapeDtypeStruct((b, t_q, D_ATT), BF16),
        scratch_shapes=[pltpu.VMEM((2, tq, 1), F32), pltpu.VMEM((2, tq, 1), F32),
                        pltpu.VMEM((2, tq, D_V), F32)],
        compiler_params=_cparams(("arbitrary", "arbitrary", "arbitrary", "arbitrary")),
        name="diff_attn",
    )(slopes, lam_vec, subln_g.reshape(1, D_V), q_arr, k_arr, v_arr)


N_AUG = 3
AUG_W = 2 * N_AUG
V_AUG_W = 2 * D_V


def _aug_lanes(lane, half, first, second):
    base = D_HEAD if half == 0 else 0
    out = jnp.zeros(first[0].shape, F32)
    for i, a in enumerate(list(first) + list(second)):
        out = jnp.where(lane == base + i, a.astype(F32), out)
    return out


def _augment(lane, half, feat, first, second):
    return jnp.where(_feat_lanes(lane, half), feat.astype(F32),
                     _aug_lanes(lane, half, first, second)).astype(BF16)


def _feat_lanes(lane, half):
    return (lane < D_HEAD) if half == 0 else (lane >= D_HEAD)


def _attn_seq_kernel(slopes_ref, lam_ref, subg_ref, q_ref, k_ref, v_ref, o_ref,
                     ka_sc, va_sc, qa_sc, s_buf, m_sc, mslot_sc, alpha_sc, acc_sc,
                     *, tq, t, lam_init):
    hd = pl.program_id(1)
    qi = pl.program_id(2)
    slope2 = slopes_ref[hd] * LOG2E
    rows_b = 512
    unroll = 2

    @pl.when(qi == 0)
    def _():
        def build(c, carry):
            r0 = pl.multiple_of(c * rows_b, rows_b)
            lane = lax.broadcasted_iota(jnp.int32, (rows_b, LANES), 1)
            pos = (lax.broadcasted_iota(jnp.int32, (rows_b, LANES), 0) + r0).astype(F32)
            parts = _split3(slope2 * pos)
            ones = [jnp.ones((rows_b, LANES), BF16)] * N_AUG
            kk = k_ref[0, pl.ds(r0, rows_b), :]
            for half in range(2):
                ka_sc[half, pl.ds(r0, rows_b), :] = _augment(lane, half, kk, parts, ones)
            va_sc[pl.ds(r0, rows_b), 0:D_V] = v_ref[0, pl.ds(r0, rows_b), :]
            va_sc[pl.ds(r0, rows_b), D_V:] = jnp.where(lane == 0, 1.0, 0.0).astype(BF16)
            return carry
        lax.fori_loop(0, t // rows_b, build, 0)

    lane_q = lax.broadcasted_iota(jnp.int32, (tq, LANES), 1)
    qpos = (lax.broadcasted_iota(jnp.int32, (tq, LANES), 0) + qi * tq).astype(F32)
    q_parts = _split3(-slope2 * qpos)
    q_ones = [jnp.ones((tq, LANES), BF16)] * N_AUG
    qq = q_ref[0]
    for half in range(2):
        qa_sc[half] = _augment(lane_q, half, qq, q_ones, q_parts)
    m_sc[...] = jnp.full(m_sc.shape, NEG, F32)
    acc_sc[...] = jnp.zeros(acc_sc.shape, F32)

    def stage_a(j, slot, diag):
        k0 = pl.multiple_of(j * tq, tq)
        for half in range(2):
            s = _dot_nt(qa_sc[half], ka_sc[half, pl.ds(k0, tq), :])
            if diag:
                ri = lax.broadcasted_iota(jnp.int32, (tq, tq), 0)
                cj = lax.broadcasted_iota(jnp.int32, (tq, tq), 1)
                s = s - (2.0 * slope2) * jnp.maximum(cj - ri, 0).astype(F32)
                vis = jnp.right_shift(cj, CHUNK_SHIFT) <= jnp.right_shift(ri, CHUNK_SHIFT)
                s = jnp.where(vis, s, NEG)
            s_buf[slot, half] = s
            mx = s[:, 0:LANES]
            for c in range(1, tq // LANES):
                mx = jnp.maximum(mx, s[:, c * LANES:(c + 1) * LANES])
            m_old = m_sc[half]
            m_new = jnp.maximum(m_old, jnp.max(mx, axis=-1, keepdims=True))
            alpha_sc[slot, half] = jnp.exp2(m_old - m_new)
            mslot_sc[slot, half] = m_new
            m_sc[half] = m_new

    def stage_b(j, slot):
        k0 = pl.multiple_of(j * tq, tq)
        vblk = va_sc[pl.ds(k0, tq), :]
        for half in range(2):
            mn = mslot_sc[slot, half]
            p = jnp.concatenate(
                [jnp.exp2(s_buf[slot, half, :, c * LANES:(c + 1) * LANES] - mn).astype(BF16)
                 for c in range(tq // LANES)], axis=-1)
            al = alpha_sc[slot, half]
            acc_sc[half] = jnp.concatenate([al, al], axis=-1) * acc_sc[half] + _dot(p, vblk)

    @pl.when(qi == 0)
    def _():
        stage_a(0, 0, True)
        stage_b(0, 0)

    @pl.when(qi > 0)
    def _():
        stage_a(0, 0, False)

        n_it = qi - 1
        n_main = n_it // unroll

        def body_main(i, carry):
            for u in range(unroll):
                j = i * unroll + u
                stage_a(j + 1, (u + 1) % 2, False)
                stage_b(j, u % 2)
            return carry

        def body_rest(j, carry):
            stage_a(j + 1, (j + 1) % 2, False)
            stage_b(j, j % 2)
            return carry
        lax.fori_loop(0, n_main, body_main, 0)
        lax.fori_loop(n_main * unroll, n_it, body_rest, 0)
        stage_a(qi, qi % 2, True)
        stage_b(qi - 1, (qi - 1) % 2)
        stage_b(qi, qi % 2)

    lv = lam_ref[...]
    lam = (jnp.exp(jnp.sum(lv[0:1] * lv[1:2], axis=-1, keepdims=True))
           - jnp.exp(jnp.sum(lv[2:3] * lv[3:4], axis=-1, keepdims=True)) + lam_init)
    o = (acc_sc[0, :, 0:D_V] / acc_sc[0, :, D_V:D_V + 1]
         - lam * (acc_sc[1, :, 0:D_V] / acc_sc[1, :, D_V:D_V + 1]))
    o_ref[0] = (_rms(o, subg_ref[...]) * (1.0 - lam_init)).astype(o_ref.dtype)


def _attention_seq(qkv, lam_vec, subln_g, lam_init, *, tq):
    b, t = qkv.shape[0], qkv.shape[1]
    assert t % tq == 0 and tq % CHUNK == 0 and t % 512 == 0
    slopes = (2.0 ** (-8.0 * jnp.arange(1, H_ATT + 1, dtype=F32) / H_ATT)).astype(F32)
    kern = functools.partial(_attn_seq_kernel, tq=tq, t=t, lam_init=lam_init)
    return pl.pallas_call(
        kern,
        grid=(b, H_ATT, t // tq),
        in_specs=[pl.BlockSpec(memory_space=pltpu.SMEM),
                  pl.BlockSpec((4, D_HEAD), lambda bb, h, qi: (0, 0)),
                  pl.BlockSpec((1, D_V), lambda bb, h, qi: (0, 0)),
                  pl.BlockSpec((1, tq, D_V), lambda bb, h, qi: (bb, qi, h)),
                  pl.BlockSpec((1, t, D_V), lambda bb, h, qi: (bb, 0, H_ATT + h)),
                  pl.BlockSpec((1, t, D_V), lambda bb, h, qi: (bb, 0, 2 * H_ATT + h))],
        out_specs=pl.BlockSpec((1, tq, D_V), lambda bb, h, qi: (bb, qi, h)),
        out_shape=jax.ShapeDtypeStruct((b, t, D_ATT), BF16),
        scratch_shapes=[pltpu.VMEM((2, t, LANES), BF16),
                        pltpu.VMEM((t, V_AUG_W), BF16),
                        pltpu.VMEM((2, tq, LANES), BF16),
                        pltpu.VMEM((2, 2, tq, tq), F32),
                        pltpu.VMEM((2, tq, LANES), F32),
                        pltpu.VMEM((2, 2, tq, LANES), F32),
                        pltpu.VMEM((2, 2, tq, LANES), F32),
                        pltpu.VMEM((2, tq, V_AUG_W), F32)],
        compiler_params=_cparams(("arbitrary", "arbitrary", "arbitrary")),
        name="diff_attn_seq",
    )(slopes, lam_vec, subln_g.reshape(1, D_V), qkv, qkv, qkv)


def _ssd_kernel(x_ref, g_ref, sc_ref, sh_ref, w_ref, convp_ref, ssmp_ref, cw_ref, cb_ref, dtb_ref,
                alog_ref, dskip_ref, ng_ref, tri_ref, e_ref, et_ref,
                y_ref, convo_ref, ssmo_ref, cbuf, state, ybuf, *, tl, n_valid):
    t = pl.program_id(1)

    @pl.when(t == 0)
    def _():
        cbuf[0:SUBLANES, :] = convp_ref[0]
        state[...] = ssmp_ref[0]

    h = _norm_mod(x_ref[0], g_ref[...], sc_ref[0], sh_ref[0]).astype(BF16)
    zxd = _dot(h, w_ref[...])
    z = zxd[:, 0:D_INNER]
    raw = zxd[:, D_INNER:D_INNER + D_XBC]
    dt_raw = zxd[:, D_INNER + D_XBC:]

    cbuf[SUBLANES:SUBLANES + tl, :] = raw
    conv = raw * cw_ref[SSM_CONV - 1:SSM_CONV, :] + cb_ref[...]
    for j in range(1, SSM_CONV):
        conv = conv + cbuf[SUBLANES - j:SUBLANES - j + tl, :] * cw_ref[SSM_CONV - 1 - j:SSM_CONV - j, :]
    convo_ref[0] = cbuf[SUBLANES + n_valid - (SSM_CONV - 1):SUBLANES + n_valid, :]
    cbuf[0:SUBLANES, :] = cbuf[tl:tl + SUBLANES, :]
    xbc = _silu(conv)
    xs = xbc[:, 0:D_INNER]
    bmat = xbc[:, D_INNER:D_INNER + N_GROUPS * D_STATE].astype(BF16)
    cmat = xbc[:, D_INNER + N_GROUPS * D_STATE:].astype(BF16)

    u = dt_raw + dtb_ref[...]
    dt = jnp.maximum(u, 0.0) + jnp.log1p(jnp.exp(-jnp.abs(u)))
    if n_valid < tl:
        rows = lax.broadcasted_iota(jnp.int32, dt.shape, 0)
        dt = jnp.where(rows < n_valid, dt, 0.0)
    da = dt * (-jnp.exp(alog_ref[...]))
    tri = tri_ref[...]
    acum = sum(_dot(tri, p) for p in _split3(da))
    acum_t = acum.T
    e = e_ref[...]
    dt_e = sum(_dot(p, e) for p in _split3(dt))
    acum_e = sum(_dot(p, e) for p in _split3(acum))
    xdt = xs * dt_e
    xdt_b = xdt.astype(BF16)
    last_e = acum_e[n_valid - 1:n_valid, :]
    xdec_b = (xdt * jnp.exp(last_e - acum_e)).astype(BF16)

    st = state[...]
    st_b = st.astype(BF16)
    causal = (lax.broadcasted_iota(jnp.int32, (tl, tl), 0) >= lax.broadcasted_iota(jnp.int32, (tl, tl), 1))
    lane = lax.broadcasted_iota(jnp.int32, (tl, LANES), 1)
    hpg = H_SSM // N_GROUPS
    gw = hpg * SSM_HEADDIM
    for g in range(N_GROUPS):
        bg = bmat[:, g * D_STATE:(g + 1) * D_STATE]
        cg = cmat[:, g * D_STATE:(g + 1) * D_STATE]
        cb = _dot_nt(cg, bg)
        ybuf[:, g * gw:(g + 1) * gw] = _dot_nt(cg, st_b[g * gw:(g + 1) * gw, :])
        for pr in range(hpg // 2):
            c0 = g * gw + pr * LANES
            x2 = xdt_b[:, c0:c0 + LANES]
            acc = None
            for half in range(2):
                hh = g * hpg + pr * 2 + half
                seg = acum[:, hh:hh + 1] - acum_t[hh:hh + 1, :]
                mm = (jnp.exp(jnp.where(causal, seg, NEG)) * cb).astype(BF16)
                keep = (lane < SSM_HEADDIM) if half == 0 else (lane >= SSM_HEADDIM)
                part = _dot(mm, jnp.where(keep, x2, jnp.zeros_like(x2)))
                acc = part if acc is None else acc + part
            ybuf[:, c0:c0 + LANES] = ybuf[:, c0:c0 + LANES] * jnp.exp(acum_e[:, c0:c0 + LANES]) + acc
        state[g * gw:(g + 1) * gw, :] = _dot_tn(xdec_b[:, g * gw:(g + 1) * gw], bg)

    f = jnp.exp(acum_t[:, n_valid - 1:n_valid])
    fb = jnp.broadcast_to(f, (LANES, D_STATE))
    fac = sum(_dot(et_ref[...], p) for p in _split3(fb))
    new_state = st * fac + state[...]
    state[...] = new_state
    ssmo_ref[0] = new_state

    y = ybuf[...] + dskip_ref[...] * xs
    y = y * _silu(z)
    gn = D_INNER // N_GROUPS
    for g in range(N_GROUPS):
        y_ref[0, :, g * gn:(g + 1) * gn] = _rms(y[:, g * gn:(g + 1) * gn],
                                                ng_ref[:, g * gn:(g + 1) * gn]).astype(y_ref.dtype)


def _ssd_branch(x, g, sc, sh, w_zxd, conv_past8, ssm_past, conv_w, conv_b, dt_bias, a_log, d_skip,
                norm_g, *, tl, n_valid):
    b, t, d = x.shape
    nt = t // tl
    assert n_valid == tl or nt == 1
    hp = H_SSM * SSM_HEADDIM
    tri = jnp.asarray(np.tril(np.ones((tl, tl), np.float32)), BF16)
    e_np = np.zeros((LANES, D_INNER), np.float32)
    for hh in range(H_SSM):
        e_np[hh, hh * SSM_HEADDIM:(hh + 1) * SSM_HEADDIM] = 1.0
    e = jnp.asarray(e_np, BF16)
    et = jnp.asarray(e_np.T, BF16)
    pad = lambda v: jnp.pad(v.astype(F32), (0, LANES - H_SSM)).reshape(1, LANES)
    row = lambda i, j: (i, j, 0)
    per_b = lambda i, j: (i, 0, 0)
    const2 = lambda i, j: (0, 0)
    kern = functools.partial(_ssd_kernel, tl=tl, n_valid=n_valid)
    return pl.pallas_call(
        kern,
        grid=(b, nt),
        in_specs=[pl.BlockSpec((1, tl, d), row),
                  pl.BlockSpec((1, d), const2),
                  pl.BlockSpec((1, 1, d), per_b),
                  pl.BlockSpec((1, 1, d), per_b),
                  pl.BlockSpec((d, D_ZXD), const2),
                  pl.BlockSpec((1, SUBLANES, D_XBC), per_b),
                  pl.BlockSpec((1, hp, D_STATE), per_b),
                  pl.BlockSpec((SSM_CONV, D_XBC), const2),
                  pl.BlockSpec((1, D_XBC), const2),
                  pl.BlockSpec((1, LANES), const2),
                  pl.BlockSpec((1, LANES), const2),
                  pl.BlockSpec((1, D_INNER), const2),
                  pl.BlockSpec((1, D_INNER), const2),
                  pl.BlockSpec((tl, tl), const2),
                  pl.BlockSpec((LANES, D_INNER), const2),
                  pl.BlockSpec((D_INNER, LANES), const2)],
        out_specs=[pl.BlockSpec((1, tl, D_INNER), row),
                   pl.BlockSpec((1, SSM_CONV - 1, D_XBC), per_b),
                   pl.BlockSpec((1, hp, D_STATE), per_b)],
        out_shape=[jax.ShapeDtypeStruct((b, t, D_INNER), BF16),
                   jax.ShapeDtypeStruct((b, SSM_CONV - 1, D_XBC), F32),
                   jax.ShapeDtypeStruct((b, hp, D_STATE), F32)],
        scratch_shapes=[pltpu.VMEM((SUBLANES + tl, D_XBC), F32),
                        pltpu.VMEM((hp, D_STATE), F32),
                        pltpu.VMEM((tl, D_INNER), F32)],
        compiler_params=_cparams(("arbitrary", "arbitrary")),
        name="ssd_branch",
    )(x, g, sc, sh, w_zxd, conv_past8, ssm_past, conv_w, conv_b.reshape(1, D_XBC), pad(dt_bias),
      pad(a_log), jnp.repeat(d_skip.astype(F32), SSM_HEADDIM).reshape(1, D_INNER),
      norm_g.reshape(1, D_INNER), tri, e, et)


def _mixout_kernel(o_ref, y_ref, x_ref, gate_ref, g1_ref, w_ref, out_ref):
    r = _dot(o_ref[0], w_ref[0:D_ATT, :]) + _dot(y_ref[0], w_ref[D_ATT:, :])
    out_ref[0] = x_ref[0] + gate_ref[0] * _rms(r, g1_ref[...])


def _mix_out(o, y, x, gate, g1, w):
    b, t, d = x.shape
    tm = min(t, 512)
    row = lambda i, j: (i, j, 0)
    per_b = lambda i, j: (i, 0, 0)
    return pl.pallas_call(
        _mixout_kernel,
        grid=(b, t // tm),
        in_specs=[pl.BlockSpec((1, tm, D_ATT), row),
                  pl.BlockSpec((1, tm, D_INNER), row),
                  pl.BlockSpec((1, tm, d), row),
                  pl.BlockSpec((1, 1, d), per_b),
                  pl.BlockSpec((1, d), lambda i, j: (0, 0)),
                  pl.BlockSpec((D_ATT + D_INNER, d), lambda i, j: (0, 0))],
        out_specs=pl.BlockSpec((1, tm, d), row),
        out_shape=jax.ShapeDtypeStruct((b, t, d), F32),
        compiler_params=_cparams(("arbitrary", "arbitrary")),
        name="mix_out",
    )(o, y, x, gate, g1, w)


def _conf_kernel(x_ref, g_ref, sc_ref, sh_ref, gate_ref, g1_ref, win_ref, bin_ref, past_ref, dww_ref,
                 dwb_ref, lng_ref, lnb_ref, wout_ref, bout_ref, out_ref, convo_ref, sb, act, cv,
                 *, tt, rc, rg):
    t = pl.program_id(1)

    @pl.when(t == 0)
    def _():
        sb[0, 0:CONF_HIST, :] = past_ref[0]

    h = _norm_mod(x_ref[0], g_ref[...], sc_ref[0], sh_ref[0]).astype(BF16)
    lin = _dot(h, win_ref[...]) + bin_ref[...]
    u = lin[:, 0:D_MODEL] * jax.nn.sigmoid(lin[:, D_MODEL:])
    sb[0, CONF_HIST:CONF_HIST + tt, :] = u
    span = tt + CONF_HIST - SUBLANES
    for r in range(1, SUBLANES):
        sb[r, 0:span, :] = sb[0, r:r + span, :]
    convo_ref[0] = sb[0, tt + CONF_HIST - (CONF_KW - 1):tt + CONF_HIST, :]

    off0 = CONF_HIST - (CONF_KW - 1)

    ng = rg // SUBLANES
    for c in range(D_MODEL // LANES):
        cs = slice(c * LANES, (c + 1) * LANES)
        wts = [dww_ref[k, :, cs] for k in range(CONF_KW)]
        bias = dwb_ref[:, cs]

        for i in range(tt // rg):
            r0 = i * rg
            accs = [jnp.broadcast_to(bias[None], (ng, SUBLANES, LANES)), None]
            for k in range(CONF_KW):
                a, r = divmod(k + off0, SUBLANES)
                xk = sb[r, pl.ds(r0 + a * SUBLANES, rg), cs]
                term = xk.reshape(ng, SUBLANES, LANES) * wts[k][None]
                accs[k % 2] = term if accs[k % 2] is None else accs[k % 2] + term
            cv[pl.ds(r0, rg), cs] = (accs[0] + accs[1]).reshape(rg, LANES)

    def chunk(i, carry):
        r0 = pl.multiple_of(i * rc, rc)
        acc = cv[pl.ds(r0, rc), :]
        mu = jnp.mean(acc, axis=-1, keepdims=True)
        cen = acc - mu
        var = jnp.mean(cen * cen, axis=-1, keepdims=True)
        yn = cen * lax.rsqrt(var + EPS) * lng_ref[...] + lnb_ref[...]
        act[pl.ds(r0, rc), :] = _silu(yn).astype(BF16)
        return carry

    lax.fori_loop(0, tt // rc, chunk, 0)
    sb[0, 0:CONF_HIST, :] = sb[0, tt:tt + CONF_HIST, :]
    r = _dot(act[...], wout_ref[...]) + bout_ref[...]
    out_ref[0] = x_ref[0] + gate_ref[0] * _rms(r, g1_ref[...])


def _conformer(x, g, sc, sh, gate, g1, w_in, b_in, past32, dw_w, dw_b, ln_g, ln_b, w_out, b_out):
    b, t, d = x.shape
    tt = min(t, 256)
    rc = min(tt, 64)
    rg = min(tt, 64)
    row = lambda i, j: (i, j, 0)
    per_b = lambda i, j: (i, 0, 0)
    const2 = lambda i, j: (0, 0)
    vec = lambda v: v.reshape(1, -1)
    dw_w = jnp.broadcast_to(dw_w[:, None, :], (CONF_KW, SUBLANES, d))
    dw_b8 = jnp.broadcast_to(dw_b[None, :], (SUBLANES, d))
    kern = functools.partial(_conf_kernel, tt=tt, rc=rc, rg=rg)
    return pl.pallas_call(
        kern,
        grid=(b, t // tt),
        in_specs=[pl.BlockSpec((1, tt, d), row),
                  pl.BlockSpec((1, d), const2),
                  pl.BlockSpec((1, 1, d), per_b),
                  pl.BlockSpec((1, 1, d), per_b),
                  pl.BlockSpec((1, 1, d), per_b),
                  pl.BlockSpec((1, d), const2),
                  pl.BlockSpec((d, 2 * d), const2),
                  pl.BlockSpec((1, 2 * d), const2),
                  pl.BlockSpec((1, CONF_HIST, d), per_b),
                  pl.BlockSpec((CONF_KW, SUBLANES, d), lambda i, j: (0, 0, 0)),
                  pl.BlockSpec((SUBLANES, d), const2),
                  pl.BlockSpec((1, d), const2),
                  pl.BlockSpec((1, d), const2),
                  pl.BlockSpec((d, d), const2),
                  pl.BlockSpec((1, d), const2)],
        out_specs=[pl.BlockSpec((1, tt, d), row),
                   pl.BlockSpec((1, CONF_KW - 1, d), per_b)],
        out_shape=[jax.ShapeDtypeStruct((b, t, d), F32),
                   jax.ShapeDtypeStruct((b, CONF_KW - 1, d), F32)],
        scratch_shapes=[pltpu.VMEM((SUBLANES, CONF_HIST + tt + SUBLANES, d), F32),
                        pltpu.VMEM((tt, d), BF16),
                        pltpu.VMEM((tt, d), F32)],
        compiler_params=_cparams(("arbitrary", "arbitrary")),
        name="conformer",
    )(x, g, sc, sh, gate, g1, w_in, vec(b_in), past32, dw_w, dw_b8, vec(ln_g), vec(ln_b), w_out,
      vec(b_out))


def _ffn_kernel(x_ref, g2_ref, sc_ref, sh_ref, gate_ref, g3_ref, wg_ref, wu_ref, wd_ref, out_ref,
                h_sc, acc_sc, *, nf):
    f = pl.program_id(2)

    @pl.when(f == 0)
    def _():
        h_sc[...] = _norm_mod(x_ref[0], g2_ref[...], sc_ref[0], sh_ref[0]).astype(BF16)
        acc_sc[...] = jnp.zeros(acc_sc.shape, F32)

    h = h_sc[...]
    a = (_silu(_dot(h, wg_ref[...])) * _dot(h, wu_ref[...])).astype(BF16)
    acc_sc[...] += _dot(a, wd_ref[...])

    @pl.when(f == nf - 1)
    def _():
        out_ref[0] = x_ref[0] + gate_ref[0] * _rms(acc_sc[...], g3_ref[...])


def _ffn(x, g2, sc, sh, gate, g3, w_up, w_down):
    b, t, d = x.shape
    tm = min(t, 512)
    tf = D_FF // 2
    nf = D_FF // tf
    row = lambda i, j, f: (i, j, 0)
    per_b = lambda i, j, f: (i, 0, 0)
    const2 = lambda i, j, f: (0, 0)
    kern = functools.partial(_ffn_kernel, nf=nf)
    return pl.pallas_call(
        kern,
        grid=(b, t // tm, nf),
        in_specs=[pl.BlockSpec((1, tm, d), row),
                  pl.BlockSpec((1, d), const2),
                  pl.BlockSpec((1, 1, d), per_b),
                  pl.BlockSpec((1, 1, d), per_b),
                  pl.BlockSpec((1, 1, d), per_b),
                  pl.BlockSpec((1, d), const2),
                  pl.BlockSpec((d, tf), lambda i, j, f: (0, f)),
                  pl.BlockSpec((d, tf), lambda i, j, f: (0, f + nf)),
                  pl.BlockSpec((tf, d), lambda i, j, f: (f, 0))],
        out_specs=pl.BlockSpec((1, tm, d), row),
        out_shape=jax.ShapeDtypeStruct((b, t, d), F32),
        scratch_shapes=[pltpu.VMEM((tm, d), BF16), pltpu.VMEM((tm, d), F32)],
        compiler_params=_cparams(("arbitrary", "arbitrary", "arbitrary")),
        name="ffn",
    )(x, g2, sc, sh, gate, g3, w_up, w_up, w_down)


def _prep_weights(w):
    hyb = w["hyb_w_in"]
    dt_cols = hyb[:, :, 3 * D_QK + D_INNER + D_XBC:]
    zxd = jnp.concatenate(
        [hyb[:, :, 3 * D_QK:3 * D_QK + D_INNER + D_XBC],
         jnp.pad(dt_cols, ((0, 0), (0, 0), (0, D_DT_PAD - H_SSM)))], axis=-1)
    return dict(
        w_qkv=hyb[:, :, 0:3 * D_QK].astype(BF16),
        w_zxd=zxd.astype(BF16),
        hyb_w_out=w["hyb_w_out"].astype(BF16),
        ffn_w_up=w["ffn_w_up"].astype(BF16),
        ffn_w_down=w["ffn_w_down"].astype(BF16),
        conf_w_in=w["conf_w_in"].astype(BF16),
        conf_w_out=w["conf_w_out"].astype(BF16),
    )


def _trunk(x, mod, past, w, wb):
    b, t, d = x.shape
    ks, vs, sconvs, ssms, cconvs = [], [], [], [], []
    for l in range(DEPTH):
        pieces = [mod[l][:, None, i * d:(i + 1) * d] for i in range(6)]
        shift_m, scale_m, gate_m, shift_f, scale_f, gate_f = pieces
        ng = lambda i: w["norm_g"][l, i].reshape(1, d)
        j = l // 2
        if l % 2 == 0:
            lam_init = 0.8 - 0.6 * math.exp(-0.3 * l)
            qkv, k_rows, v_rows = _qkv_proj(x, ng(0), scale_m, shift_m, wb["w_qkv"][j])
            if past is None:
                o = _attention_seq(qkv, w["attn_lambda"][j], w["attn_subln_g"][j], lam_init,
                                   tq=min(t, 512))
                conv_past = jnp.zeros((b, SSM_CONV - 1, D_XBC), F32)
                ssm_past = jnp.zeros((b, H_SSM * SSM_HEADDIM, D_STATE), F32)
            else:
                pos0 = past[0].shape[2]
                kv_len = pos0 + t
                kv_pad = -(-kv_len // LANES) * LANES
                zpad = jnp.zeros((b, kv_pad - kv_len, D_QK), BF16)
                k_all = jnp.concatenate([past[0][j].reshape(b, pos0, D_QK).astype(BF16),
                                         qkv[:, :, D_QK:2 * D_QK], zpad], axis=1)
                v_all = jnp.concatenate([past[1][j].reshape(b, pos0, D_ATT).astype(BF16),
                                         qkv[:, :, 2 * D_QK:], zpad], axis=1)
                o = _attention(qkv, 0, k_all, 0, v_all, 0, w["attn_lambda"][j],
                               w["attn_subln_g"][j], lam_init, pos0=pos0, kv_len=kv_len, tq=t,
                               tk=kv_pad)
                conv_past = past[2][j]
                ssm_past = past[3][j].reshape(b, H_SSM * SSM_HEADDIM, D_STATE)
            conv_past8 = jnp.pad(conv_past, ((0, 0), (SUBLANES - (SSM_CONV - 1), 0), (0, 0)))
            tl = 256 if t % 256 == 0 else LANES
            t_pad = -(-t // tl) * tl
            x_ssd = x if t_pad == t else jnp.pad(x, ((0, 0), (0, t_pad - t), (0, 0)))
            y, conv_new, ssm_new = _ssd_branch(
                x_ssd, ng(0), scale_m, shift_m, wb["w_zxd"][j], conv_past8, ssm_past,
                w["ssm_conv_w"][j], w["ssm_conv_b"][j], w["ssm_dt_bias"][j], w["ssm_a_log"][j],
                w["ssm_d"][j], w["ssm_norm_g"][j], tl=tl, n_valid=tl if t_pad == t else t)
            y = y[:, :t]
            x = _mix_out(o, y, x, gate_m, ng(1), wb["hyb_w_out"][j])
            ks.append(k_rows.reshape(b, t, H_ATT, 2 * D_HEAD))
            vs.append(v_rows.reshape(b, t, H_ATT, D_V))
            sconvs.append(conv_new)
            ssms.append(ssm_new.reshape(b, H_SSM, SSM_HEADDIM, D_STATE))
        else:
            if past is None:
                past32 = jnp.zeros((b, CONF_HIST, d), F32)
            else:
                past32 = jnp.pad(past[4][j], ((0, 0), (CONF_HIST - (CONF_KW - 1), 0), (0, 0)))
            x, conv_new = _conformer(x, ng(0), scale_m, shift_m, gate_m, ng(1), wb["conf_w_in"][j],
                                     w["conf_b_in"][j], past32, w["conf_dw_w"][j], w["conf_dw_b"][j],
                                     w["conf_ln_g"][j], w["conf_ln_b"][j], wb["conf_w_out"][j],
                                     w["conf_b_out"][j])
            cconvs.append(conv_new)
        x = _ffn(x, ng(2), scale_f, shift_f, gate_f, ng(3), wb["ffn_w_up"][l], wb["ffn_w_down"][l])
    return x, jnp.stack(ks), jnp.stack(vs), jnp.stack(sconvs), jnp.stack(ssms), jnp.stack(cconvs)


def kernel(x_prompt, x_sample, c_prompt, c_sample, cache_attn_k, cache_attn_v, state_ssm_conv, state_ssm, state_conf_conv, ada_w, ada_b, norm_g, ffn_w_up, ffn_w_down, hyb_w_in, attn_lambda, attn_subln_g, ssm_conv_w, ssm_conv_b, ssm_dt_bias, ssm_a_log, ssm_d, ssm_norm_g, hyb_w_out, conf_w_in, conf_b_in, conf_dw_w, conf_dw_b, conf_ln_g, conf_ln_b, conf_w_out, conf_b_out):
    w = dict(norm_g=norm_g, ffn_w_up=ffn_w_up, ffn_w_down=ffn_w_down, hyb_w_in=hyb_w_in,
             attn_lambda=attn_lambda, attn_subln_g=attn_subln_g, ssm_conv_w=ssm_conv_w,
             ssm_conv_b=ssm_conv_b, ssm_dt_bias=ssm_dt_bias, ssm_a_log=ssm_a_log, ssm_d=ssm_d,
             ssm_norm_g=ssm_norm_g, hyb_w_out=hyb_w_out, conf_w_in=conf_w_in, conf_b_in=conf_b_in,
             conf_dw_w=conf_dw_w, conf_dw_b=conf_dw_b, conf_ln_g=conf_ln_g, conf_ln_b=conf_ln_b,
             conf_w_out=conf_w_out, conf_b_out=conf_b_out)
    wb = _prep_weights(w)
    bp, bs = c_prompt.shape[0], c_sample.shape[0]
    rows = -(-(bp + bs) // 16) * 16
    c_all = jnp.concatenate([c_prompt, c_sample, jnp.zeros((rows - bp - bs, D_MODEL), F32)], axis=0)
    mod = _ada(c_all, ada_w, ada_b)
    y_p, k_p, v_p, sconv_p, ssm_p, cconv_p = _trunk(x_prompt, mod[:, 0:bp], None, w, wb)
    past = (cache_attn_k, cache_attn_v, state_ssm_conv, state_ssm, state_conf_conv)
    y_s, k_s, v_s, sconv_s, ssm_s, cconv_s = _trunk(x_sample, mod[:, bp:bp + bs], past, w, wb)
    return (y_p, y_s, k_p, v_p, sconv_p, ssm_p, cconv_p, k_s, v_s, sconv_s, ssm_s, cconv_s)
```

```python
import functools
import math

import ml_dtypes
import numpy as np
import jax
import jax.numpy as jnp
from jax import lax
from jax.experimental import pallas as pl
from jax.experimental.pallas import tpu as pltpu

F32 = jnp.float32
BF16 = jnp.bfloat16

D_MODEL = 1024
DEPTH = 4
CHUNK = 64
CHUNK_SHIFT = CHUNK.bit_length() - 1
assert 1 << CHUNK_SHIFT == CHUNK
EPS = 1e-6
H_ATT = 8
D_HEAD = D_MODEL // H_ATT // 2
D_V = 2 * D_HEAD
D_QK = H_ATT * 2 * D_HEAD
D_ATT = H_ATT * D_V
D_INNER = D_MODEL
SSM_HEADDIM = 64
H_SSM = D_INNER // SSM_HEADDIM
D_STATE = 128
N_GROUPS = 2
SSM_CONV = 4
D_XBC = D_INNER + 2 * N_GROUPS * D_STATE
CONF_KW = 31
D_FF = -(-8 * D_MODEL // (3 * 256)) * 256

LANES = 128
SUBLANES = 8
VMEM_LIMIT = 56 * 1024 * 1024
NEG = -1e30
LOG2E = math.log2(math.e)
QK_SCALE = D_HEAD ** -0.5 * LOG2E
D_DT_PAD = LANES
D_ZXD = D_INNER + D_XBC + D_DT_PAD
CONF_HIST = 32


def _cparams(sem):
    return pltpu.CompilerParams(dimension_semantics=sem, vmem_limit_bytes=VMEM_LIMIT)


def _dot(a, b):
    return jnp.dot(a, b, preferred_element_type=F32)


def _dot_nt(a, b):
    return lax.dot_general(a, b, (((1,), (1,)), ((), ())), preferred_element_type=F32)


def _dot_tn(a, b):
    return lax.dot_general(a, b, (((0,), (0,)), ((), ())), preferred_element_type=F32)


def _rms(xf, g):
    return xf * lax.rsqrt(jnp.mean(xf * xf, axis=-1, keepdims=True) + EPS) * g


def _norm_mod(xf, g, sc, sh):
    return _rms(xf, g) * (1.0 + sc) + sh


def _silu(x):
    return x * jax.nn.sigmoid(x)


def _split3(v):
    hi = v.astype(BF16)
    r = v - hi.astype(F32)
    mid = r.astype(BF16)
    lo = (r - mid.astype(F32)).astype(BF16)
    return hi, mid, lo


def _ada_kernel(c_ref, w_ref, b_ref, o_ref):
    c = c_ref[...]
    s = _silu(c).astype(BF16)
    o_ref[0] = _dot(s, w_ref[0].astype(BF16)) + b_ref[0]


def _ada(c, ada_w, ada_b):
    r = c.shape[0]
    tn = 1536
    return pl.pallas_call(
        _ada_kernel,
        grid=(DEPTH, 6 * D_MODEL // tn),
        in_specs=[pl.BlockSpec((r, D_MODEL), lambda l, j: (0, 0)),
                  pl.BlockSpec((1, D_MODEL, tn), lambda l, j: (l, 0, j)),
                  pl.BlockSpec((1, 1, tn), lambda l, j: (l, 0, j))],
        out_specs=pl.BlockSpec((1, r, tn), lambda l, j: (l, 0, j)),
        out_shape=jax.ShapeDtypeStruct((DEPTH, r, 6 * D_MODEL), F32),
        compiler_params=_cparams(("arbitrary", "arbitrary")),
        name="ada",
    )(c, ada_w, ada_b.reshape(DEPTH, 1, 6 * D_MODEL))


def _qkv_kernel(x_ref, g_ref, sc_ref, sh_ref, w_ref, qkv_ref, k_ref, v_ref):
    h = _norm_mod(x_ref[0], g_ref[...], sc_ref[0], sh_ref[0]).astype(BF16)
    r = _dot(h, w_ref[...])
    qkv_ref[0, :, 0:D_QK] = (r[:, 0:D_QK] * QK_SCALE).astype(BF16)
    qkv_ref[0, :, D_QK:] = r[:, D_QK:].astype(BF16)
    k_ref[0] = r[:, D_QK:2 * D_QK]
    v_ref[0] = r[:, 2 * D_QK:]


def _qkv_proj(x, g, sc, sh, w):
    b, t, d = x.shape
    tm = min(t, 512)
    n = w.shape[1]
    row = lambda i, j: (i, j, 0)
    per_b = lambda i, j: (i, 0, 0)
    return pl.pallas_call(
        _qkv_kernel,
        grid=(b, t // tm),
        in_specs=[pl.BlockSpec((1, tm, d), row),
                  pl.BlockSpec((1, d), lambda i, j: (0, 0)),
                  pl.BlockSpec((1, 1, d), per_b),
                  pl.BlockSpec((1, 1, d), per_b),
                  pl.BlockSpec((d, n), lambda i, j: (0, 0))],
        out_specs=[pl.BlockSpec((1, tm, n), row),
                   pl.BlockSpec((1, tm, D_QK), row),
                   pl.BlockSpec((1, tm, D_ATT), row)],
        out_shape=[jax.ShapeDtypeStruct((b, t, n), BF16),
                   jax.ShapeDtypeStruct((b, t, D_QK), F32),
                   jax.ShapeDtypeStruct((b, t, D_ATT), F32)],
        compiler_params=_cparams(("arbitrary", "arbitrary")),
        name="qkv_proj",
    )(x, g, sc, sh, w)


def _attn_kernel(slopes_ref, lam_ref, subg_ref, q_ref, k_ref, v_ref, o_ref, m_sc, l_sc, acc_sc,
                 *, tq, tk, pos0, kv_len, lam_init, nk):
    hd = pl.program_id(1)
    qi = pl.program_id(2)
    ki = pl.program_id(3)
    slope = slopes_ref[hd]
    qs = pos0 + qi * tq
    ks = ki * tk
    last_chunk_end = ((qs + tq - 1) // CHUNK) * CHUNK + CHUNK - 1
    last_needed = jnp.minimum(nk - 1, last_chunk_end // tk)

    @pl.when(ki == 0)
    def _():
        m_sc[...] = jnp.full(m_sc.shape, NEG, F32)
        l_sc[...] = jnp.zeros(l_sc.shape, F32)
        acc_sc[...] = jnp.zeros(acc_sc.shape, F32)

    def update(bias, mask):
        q = q_ref[0]
        k = k_ref[0]
        v = v_ref[0]
        for m in range(2):
            s = (_dot_nt(q[:, m * D_HEAD:(m + 1) * D_HEAD], k[:, m * D_HEAD:(m + 1) * D_HEAD])
                 + LOG2E * bias)
            if mask is not None:
                s = jnp.where(mask, s, NEG)
            m_old = m_sc[m]
            m_new = jnp.maximum(m_old, jnp.max(s, axis=-1, keepdims=True))
            alpha = jnp.exp2(m_old - m_new)
            p = jnp.exp2(s - m_new)
            l_sc[m] = alpha * l_sc[m] + jnp.sum(p, axis=-1, keepdims=True)
            acc_sc[m] = alpha * acc_sc[m] + _dot(p.astype(BF16), v)
            m_sc[m] = m_new

    interior = jnp.logical_and(ks + tk - 1 <= qs, ks + tk <= kv_len)

    @pl.when(jnp.logical_and(ki <= last_needed, interior))
    def _():
        col = lax.broadcasted_iota(jnp.int32, (1, tk), 1) + (ks - qs)
        update(slope * col.astype(F32), None)

    @pl.when(jnp.logical_and(ki <= last_needed, jnp.logical_not(interior)))
    def _():
        row = lax.broadcasted_iota(jnp.int32, (tq, tk), 0)
        qpos = row + qs
        kpos = lax.broadcasted_iota(jnp.int32, (tq, tk), 1) + ks
        bias = slope * (row - jnp.abs(qpos - kpos)).astype(F32)
        mask = jnp.logical_and(jnp.right_shift(kpos, CHUNK_SHIFT) <= jnp.right_shift(qpos, CHUNK_SHIFT),
                               kpos < kv_len)
        update(bias, mask)

    @pl.when(ki == nk - 1)
    def _():
        lv = lam_ref[...]
        lam = (jnp.exp(jnp.sum(lv[0:1] * lv[1:2], axis=-1, keepdims=True))
               - jnp.exp(jnp.sum(lv[2:3] * lv[3:4], axis=-1, keepdims=True)) + lam_init)
        o = acc_sc[0] / l_sc[0] - lam * (acc_sc[1] / l_sc[1])
        o = _rms(o, subg_ref[...]) * (1.0 - lam_init)
        o_ref[0] = o.astype(o_ref.dtype)


def _attention(q_arr, q_col0, k_arr, k_col0, v_arr, v_col0, lam_vec, subln_g, lam_init,
               *, pos0, kv_len, tq, tk):
    b, t_q = q_arr.shape[0], q_arr.shape[1]
    t_k = k_arr.shape[1]
    nq, nk = t_q // tq, t_k // tk
    slopes = _alibi_slopes()

    def kv_block(qi, ki):
        last = ((pos0 + qi * tq + tq - 1) // CHUNK * CHUNK + CHUNK - 1) // tk
        return jnp.minimum(ki, jnp.minimum(last, nk - 1))

    kern = functools.partial(_attn_kernel, tq=tq, tk=tk, pos0=pos0, kv_len=kv_len,
                             lam_init=lam_init, nk=nk)
    return pl.pallas_call(
        kern,
        grid=(b, H_ATT, nq, nk),
        in_specs=[pl.BlockSpec(memory_space=pltpu.SMEM),
                  pl.BlockSpec((4, D_HEAD), lambda bb, h, qi, ki: (0, 0)),
                  pl.BlockSpec((1, D_V), lambda bb, h, qi, ki: (0, 0)),
                  pl.BlockSpec((1, tq, D_V), lambda bb, h, qi, ki: (bb, qi, q_col0 + h)),
                  pl.BlockSpec((1, tk, D_V), lambda bb, h, qi, ki: (bb, kv_block(qi, ki), k_col0 + h)),
                  pl.BlockSpec((1, tk, D_V), lambda bb, h, qi, ki: (bb, kv_block(qi, ki), v_col0 + h))],
        out_specs=pl.BlockSpec((1, tq, D_V), lambda bb, h, qi, ki: (bb, qi, h)),
        out_shape=jax.ShapeDtypeStruct((b, t_q, D_ATT), BF16),
        scratch_shapes=[pltpu.VMEM((2, tq, 1), F32), pltpu.VMEM((2, tq, 1), F32),
                        pltpu.VMEM((2, tq, D_V), F32)],
        compiler_params=_cparams(("arbitrary", "arbitrary", "arbitrary", "arbitrary")),
        name="diff_attn",
    )(slopes, lam_vec, subln_g.reshape(1, D_V), q_arr, k_arr, v_arr)


N_AUG = 3
AUG_W = 2 * N_AUG
V_AUG_W = 2 * D_V


def _alibi_slopes():
    return 2.0 ** (-8.0 * jnp.arange(1, H_ATT + 1, dtype=F32) / H_ATT)


def _alibi_tables(t):
    slopes = np.float32(2.0) ** (np.float32(-8.0) * np.arange(1, H_ATT + 1, dtype=np.float32) / H_ATT)
    bias = (slopes * np.float32(LOG2E))[:, None] * np.arange(t, dtype=np.float32)[None, :]

    def split(v):
        terms = []
        for _ in range(N_AUG):
            part = v.astype(ml_dtypes.bfloat16)
            terms.append(part)
            v = v - part.astype(np.float32)
        return terms

    ones = [np.ones(bias.shape, ml_dtypes.bfloat16)] * N_AUG
    k_terms = jnp.asarray(np.stack(split(bias) + ones, axis=-1))
    q_terms = jnp.asarray(np.stack(ones + split(-bias), axis=-1))

    def spread(terms):
        halves = [jnp.pad(terms, ((0, 0), (0, 0), (base, LANES - base - AUG_W))) for base in (D_HEAD, 0)]
        return jnp.stack(halves, axis=1)

    return spread(k_terms), spread(q_terms)


def _fill_features(lane, half, feat, table):
    keep = (lane < D_HEAD) if half == 0 else (lane >= D_HEAD)
    return jnp.where(keep, feat.astype(F32), table.astype(F32)).astype(BF16)


def _attn_seq_kernel(slopes_ref, lam_ref, subg_ref, q_ref, k_ref, v_ref, kt_ref, qt_ref, o_ref,
                     ka_sc, va_sc, qa_sc, corr_sc, s_buf, m_sc, mslot_sc, alpha_sc, acc_sc,
                     *, tq, t, lam_init):
    hd = pl.program_id(1)
    qi = pl.program_id(2)
    rows_b = 512
    unroll = 2

    @pl.when(qi == 0)
    def _():
        def build(c, carry):
            r0 = pl.multiple_of(c * rows_b, rows_b)
            lane = lax.broadcasted_iota(jnp.int32, (rows_b, LANES), 1)
            kk = k_ref[0, pl.ds(r0, rows_b), :]
            for half in range(2):
                ka_sc[half, pl.ds(r0, rows_b), :] = _fill_features(
                    lane, half, kk, kt_ref[0, half, pl.ds(r0, rows_b), :])
            va_sc[pl.ds(r0, rows_b), 0:D_V] = v_ref[0, pl.ds(r0, rows_b), :]
            va_sc[pl.ds(r0, rows_b), D_V:] = jnp.where(lane == 0, 1.0, 0.0).astype(BF16)
            return carry
        lax.fori_loop(0, t // rows_b, build, 0)
        ri = lax.broadcasted_iota(jnp.int32, (tq, tq), 0)
        cj = lax.broadcasted_iota(jnp.int32, (tq, tq), 1)
        fix = (-2.0 * LOG2E * slopes_ref[hd]) * jnp.maximum(cj - ri, 0).astype(F32)
        vis = jnp.right_shift(cj, CHUNK_SHIFT) <= jnp.right_shift(ri, CHUNK_SHIFT)
        corr_sc[...] = jnp.where(vis, fix, NEG)

    lane_q = lax.broadcasted_iota(jnp.int32, (tq, LANES), 1)
    qq = q_ref[0]
    for half in range(2):
        qa_sc[half] = _fill_features(lane_q, half, qq, qt_ref[0, half])
    m_sc[...] = jnp.full(m_sc.shape, NEG, F32)
    acc_sc[...] = jnp.zeros(acc_sc.shape, F32)

    def stage_a(j, slot, diag):
        k0 = pl.multiple_of(j * tq, tq)
        for half in range(2):
            s = _dot_nt(qa_sc[half], ka_sc[half, pl.ds(k0, tq), :])
            if diag:
                s = s + corr_sc[...]
            s_buf[slot, half] = s
            mx = s[:, 0:LANES]
            for c in range(1, tq // LANES):
                mx = jnp.maximum(mx, s[:, c * LANES:(c + 1) * LANES])
            m_old = m_sc[half]
            m_new = jnp.maximum(m_old, jnp.max(mx, axis=-1, keepdims=True))
            alpha_sc[slot, half] = jnp.exp2(m_old - m_new)
            mslot_sc[slot, half] = m_new
            m_sc[half] = m_new

    def stage_b(j, slot):
        k0 = pl.multiple_of(j * tq, tq)
        vblk = va_sc[pl.ds(k0, tq), :]
        for half in range(2):
            mn = mslot_sc[slot, half]
            p = jnp.concatenate(
                [jnp.exp2(s_buf[slot, half, :, c * LANES:(c + 1) * LANES] - mn).astype(BF16)
                 for c in range(tq // LANES)], axis=-1)
            al = alpha_sc[slot, half]
            acc_sc[half] = jnp.concatenate([al, al], axis=-1) * acc_sc[half] + _dot(p, vblk)

    @pl.when(qi == 0)
    def _():
        stage_a(0, 0, True)
        stage_b(0, 0)

    @pl.when(qi > 0)
    def _():
        stage_a(0, 0, False)

        n_it = qi - 1
        n_main = n_it // unroll

        def body_main(i, carry):
            for u in range(unroll):
                j = i * unroll + u
                stage_a(j + 1, (u + 1) % 2, False)
                stage_b(j, u % 2)
            return carry

        def body_rest(j, carry):
            stage_a(j + 1, (j + 1) % 2, False)
            stage_b(j, j % 2)
            return carry
        lax.fori_loop(0, n_main, body_main, 0)
        lax.fori_loop(n_main * unroll, n_it, body_rest, 0)
        stage_a(qi, qi % 2, True)
        stage_b(qi - 1, (qi - 1) % 2)
        stage_b(qi, qi % 2)

    lv = lam_ref[...]
    lam = (jnp.exp(jnp.sum(lv[0:1] * lv[1:2], axis=-1, keepdims=True))
           - jnp.exp(jnp.sum(lv[2:3] * lv[3:4], axis=-1, keepdims=True)) + lam_init)
    o = (acc_sc[0, :, 0:D_V] / acc_sc[0, :, D_V:D_V + 1]
         - lam * (acc_sc[1, :, 0:D_V] / acc_sc[1, :, D_V:D_V + 1]))
    o_ref[0] = (_rms(o, subg_ref[...]) * (1.0 - lam_init)).astype(o_ref.dtype)


def _attention_seq(qkv, lam_vec, subln_g, lam_init, *, tq):
    b, t = qkv.shape[0], qkv.shape[1]
    assert t % tq == 0 and tq % CHUNK == 0 and t % 512 == 0
    kt, qt = _alibi_tables(t)
    kern = functools.partial(_attn_seq_kernel, tq=tq, t=t, lam_init=lam_init)
    return pl.pallas_call(
        kern,
        grid=(b, H_ATT, t // tq),
        in_specs=[pl.BlockSpec(memory_space=pltpu.SMEM),
                  pl.BlockSpec((4, D_HEAD), lambda bb, h, qi: (0, 0)),
                  pl.BlockSpec((1, D_V), lambda bb, h, qi: (0, 0)),
                  pl.BlockSpec((1, tq, D_V), lambda bb, h, qi: (bb, qi, h)),
                  pl.BlockSpec((1, t, D_V), lambda bb, h, qi: (bb, 0, H_ATT + h)),
                  pl.BlockSpec((1, t, D_V), lambda bb, h, qi: (bb, 0, 2 * H_ATT + h)),
                  pl.BlockSpec((1, 2, t, LANES), lambda bb, h, qi: (h, 0, 0, 0)),
                  pl.BlockSpec((1, 2, tq, LANES), lambda bb, h, qi: (h, 0, qi, 0))],
        out_specs=pl.BlockSpec((1, tq, D_V), lambda bb, h, qi: (bb, qi, h)),
        out_shape=jax.ShapeDtypeStruct((b, t, D_ATT), BF16),
        scratch_shapes=[pltpu.VMEM((2, t, LANES), BF16),
                        pltpu.VMEM((t, V_AUG_W), BF16),
                        pltpu.VMEM((2, tq, LANES), BF16),
                        pltpu.VMEM((tq, tq), F32),
                        pltpu.VMEM((2, 2, tq, tq), F32),
                        pltpu.VMEM((2, tq, LANES), F32),
                        pltpu.VMEM((2, 2, tq, LANES), F32),
                        pltpu.VMEM((2, 2, tq, LANES), F32),
                        pltpu.VMEM((2, tq, V_AUG_W), F32)],
        compiler_params=_cparams(("arbitrary", "arbitrary", "arbitrary")),
        name="diff_attn_seq",
    )(_alibi_slopes(), lam_vec, subln_g.reshape(1, D_V), qkv, qkv, qkv, kt, qt)


def _ssd_kernel(x_ref, g_ref, sc_ref, sh_ref, w_ref, convp_ref, ssmp_ref, cw_ref, cb_ref, dtb_ref,
                alog_ref, dskip_ref, ng_ref, tri_ref, e_ref,
                y_ref, convo_ref, ssmo_ref, cbuf, state, ybuf, *, tl, n_valid):
    t = pl.program_id(1)

    @pl.when(t == 0)
    def _():
        cbuf[0:SUBLANES, :] = convp_ref[0]
        state[...] = ssmp_ref[0]

    h = _norm_mod(x_ref[0], g_ref[...], sc_ref[0], sh_ref[0]).astype(BF16)
    zxd = _dot(h, w_ref[...])
    z = zxd[:, 0:D_INNER]
    raw = zxd[:, D_INNER:D_INNER + D_XBC]
    dt_raw = zxd[:, D_INNER + D_XBC:]

    cbuf[SUBLANES:SUBLANES + tl, :] = raw
    conv = raw * cw_ref[SSM_CONV - 1:SSM_CONV, :] + cb_ref[...]
    for j in range(1, SSM_CONV):
        conv = conv + cbuf[SUBLANES - j:SUBLANES - j + tl, :] * cw_ref[SSM_CONV - 1 - j:SSM_CONV - j, :]
    convo_ref[0] = cbuf[SUBLANES + n_valid - (SSM_CONV - 1):SUBLANES + n_valid, :]
    cbuf[0:SUBLANES, :] = cbuf[tl:tl + SUBLANES, :]
    xbc = _silu(conv)
    xs = xbc[:, 0:D_INNER]
    bmat = xbc[:, D_INNER:D_INNER + N_GROUPS * D_STATE].astype(BF16)
    cmat = xbc[:, D_INNER + N_GROUPS * D_STATE:].astype(BF16)

    u = dt_raw + dtb_ref[...]
    dt = jnp.maximum(u, 0.0) + jnp.log1p(jnp.exp(-jnp.abs(u)))
    if n_valid < tl:
        rows = lax.broadcasted_iota(jnp.int32, dt.shape, 0)
        dt = jnp.where(rows < n_valid, dt, 0.0)
    da = dt * (-jnp.exp(alog_ref[...]))
    tri = tri_ref[...]
    acum = sum(_dot(tri, p) for p in _split3(da))
    acum_t = acum.T
    e = e_ref[...]
    dt_e = sum(_dot(p, e) for p in _split3(dt)[:2])
    acum_e = sum(_dot(p, e) for p in _split3(acum))
    xdt = xs * dt_e
    xdt_b = xdt.astype(BF16)
    last_e = acum_e[n_valid - 1:n_valid, :]
    xdec_b = (xdt * jnp.exp(last_e - acum_e)).astype(BF16)

    st = state[...]
    st_b = st.astype(BF16)
    causal = (lax.broadcasted_iota(jnp.int32, (tl, tl), 0) >= lax.broadcasted_iota(jnp.int32, (tl, tl), 1))
    lane = lax.broadcasted_iota(jnp.int32, (tl, LANES), 1)
    hpg = H_SSM // N_GROUPS
    gw = hpg * SSM_HEADDIM
    for g in range(N_GROUPS):
        bg = bmat[:, g * D_STATE:(g + 1) * D_STATE]
        cg = cmat[:, g * D_STATE:(g + 1) * D_STATE]
        cb = _dot_nt(cg, bg)
        ybuf[:, g * gw:(g + 1) * gw] = _dot_nt(cg, st_b[g * gw:(g + 1) * gw, :])
        for pr in range(hpg // 2):
            c0 = g * gw + pr * LANES
            x2 = xdt_b[:, c0:c0 + LANES]
            acc = None
            for half in range(2):
                hh = g * hpg + pr * 2 + half
                seg = acum[:, hh:hh + 1] - acum_t[hh:hh + 1, :]
                mm = (jnp.exp(jnp.where(causal, seg, NEG)) * cb).astype(BF16)
                keep = (lane < SSM_HEADDIM) if half == 0 else (lane >= SSM_HEADDIM)
                part = _dot(mm, jnp.where(keep, x2, jnp.zeros_like(x2)))
                acc = part if acc is None else acc + part
            ybuf[:, c0:c0 + LANES] = ybuf[:, c0:c0 + LANES] * jnp.exp(acum_e[:, c0:c0 + LANES]) + acc
        state[g * gw:(g + 1) * gw, :] = _dot_tn(xdec_b[:, g * gw:(g + 1) * gw], bg)

    fb = jnp.broadcast_to(jnp.exp(acum_t[:, n_valid - 1:n_valid]), (LANES, D_STATE))
    for hh in range(H_SSM):
        rows = slice(hh * SSM_HEADDIM, (hh + 1) * SSM_HEADDIM)
        new_rows = st[rows, :] * fb[hh:hh + 1, :] + state[rows, :]
        state[rows, :] = new_rows
        ssmo_ref[0, rows, :] = new_rows

    y = ybuf[...] + dskip_ref[...] * xs
    y = y * _silu(z)
    gn = D_INNER // N_GROUPS
    for g in range(N_GROUPS):
        y_ref[0, :, g * gn:(g + 1) * gn] = _rms(y[:, g * gn:(g + 1) * gn],
                                                ng_ref[:, g * gn:(g + 1) * gn]).astype(y_ref.dtype)


def _ssd_branch(x, g, sc, sh, w_zxd, conv_past8, ssm_past, conv_w, conv_b, dt_bias, a_log, d_skip,
                norm_g, *, tl, n_valid):
    b, t, d = x.shape
    nt = t // tl
    assert n_valid == tl or nt == 1
    hp = H_SSM * SSM_HEADDIM
    tri = jnp.asarray(np.tril(np.ones((tl, tl), np.float32)), BF16)
    e_np = np.zeros((LANES, D_INNER), np.float32)
    for hh in range(H_SSM):
        e_np[hh, hh * SSM_HEADDIM:(hh + 1) * SSM_HEADDIM] = 1.0
    e = jnp.asarray(e_np, BF16)
    pad = lambda v: jnp.pad(v.astype(F32), (0, LANES - H_SSM)).reshape(1, LANES)
    row = lambda i, j: (i, j, 0)
    per_b = lambda i, j: (i, 0, 0)
    const2 = lambda i, j: (0, 0)
    kern = functools.partial(_ssd_kernel, tl=tl, n_valid=n_valid)
    return pl.pallas_call(
        kern,
        grid=(b, nt),
        in_specs=[pl.BlockSpec((1, tl, d), row),
                  pl.BlockSpec((1, d), const2),
                  pl.BlockSpec((1, 1, d), per_b),
                  pl.BlockSpec((1, 1, d), per_b),
                  pl.BlockSpec((d, D_ZXD), const2),
                  pl.BlockSpec((1, SUBLANES, D_XBC), per_b),
                  pl.BlockSpec((1, hp, D_STATE), per_b),
                  pl.BlockSpec((SSM_CONV, D_XBC), const2),
                  pl.BlockSpec((1, D_XBC), const2),
                  pl.BlockSpec((1, LANES), const2),
                  pl.BlockSpec((1, LANES), const2),
                  pl.BlockSpec((1, D_INNER), const2),
                  pl.BlockSpec((1, D_INNER), const2),
                  pl.BlockSpec((tl, tl), const2),
                  pl.BlockSpec((LANES, D_INNER), const2)],
        out_specs=[pl.BlockSpec((1, tl, D_INNER), row),
                   pl.BlockSpec((1, SSM_CONV - 1, D_XBC), per_b),
                   pl.BlockSpec((1, hp, D_STATE), per_b)],
        out_shape=[jax.ShapeDtypeStruct((b, t, D_INNER), BF16),
                   jax.ShapeDtypeStruct((b, SSM_CONV - 1, D_XBC), F32),
                   jax.ShapeDtypeStruct((b, hp, D_STATE), F32)],
        scratch_shapes=[pltpu.VMEM((SUBLANES + tl, D_XBC), F32),
                        pltpu.VMEM((hp, D_STATE), F32),
                        pltpu.VMEM((tl, D_INNER), F32)],
        compiler_params=_cparams(("arbitrary", "arbitrary")),
        name="ssd_branch",
    )(x, g, sc, sh, w_zxd, conv_past8, ssm_past, conv_w, conv_b.reshape(1, D_XBC), pad(dt_bias),
      pad(a_log), jnp.repeat(d_skip.astype(F32), SSM_HEADDIM).reshape(1, D_INNER),
      norm_g.reshape(1, D_INNER), tri, e)


def _mixout_kernel(o_ref, y_ref, x_ref, gate_ref, g1_ref, w_ref, out_ref):
    r = _dot(o_ref[0], w_ref[0:D_ATT, :]) + _dot(y_ref[0], w_ref[D_ATT:, :])
    out_ref[0] = x_ref[0] + gate_ref[0] * _rms(r, g1_ref[...])


def _mix_out(o, y, x, gate, g1, w):
    b, t, d = x.shape
    tm = min(t, 512)
    row = lambda i, j: (i, j, 0)
    per_b = lambda i, j: (i, 0, 0)
    return pl.pallas_call(
        _mixout_kernel,
        grid=(b, t // tm),
        in_specs=[pl.BlockSpec((1, tm, D_ATT), row),
                  pl.BlockSpec((1, tm, D_INNER), row),
                  pl.BlockSpec((1, tm, d), row),
                  pl.BlockSpec((1, 1, d), per_b),
                  pl.BlockSpec((1, d), lambda i, j: (0, 0)),
                  pl.BlockSpec((D_ATT + D_INNER, d), lambda i, j: (0, 0))],
        out_specs=pl.BlockSpec((1, tm, d), row),
        out_shape=jax.ShapeDtypeStruct((b, t, d), F32),
        compiler_params=_cparams(("arbitrary", "arbitrary")),
        name="mix_out",
    )(o, y, x, gate, g1, w)


def _conf_kernel(x_ref, g_ref, sc_ref, sh_ref, gate_ref, g1_ref, win_ref, bin_ref, past_ref, dww_ref,
                 dwb_ref, lng_ref, lnb_ref, wout_ref, bout_ref, out_ref, convo_ref, sb, act, cv,
                 *, tt, rc, rg):
    t = pl.program_id(1)

    @pl.when(t == 0)
    def _():
        sb[0, 0:CONF_HIST, :] = past_ref[0]

    h = _norm_mod(x_ref[0], g_ref[...], sc_ref[0], sh_ref[0]).astype(BF16)
    lin = _dot(h, win_ref[...]) + bin_ref[...]
    u = lin[:, 0:D_MODEL] * jax.nn.sigmoid(lin[:, D_MODEL:])
    sb[0, CONF_HIST:CONF_HIST + tt, :] = u
    span = tt + CONF_HIST - SUBLANES
    for r in range(1, SUBLANES):
        sb[r, 0:span, :] = sb[0, r:r + span, :]
    convo_ref[0] = sb[0, tt + CONF_HIST - (CONF_KW - 1):tt + CONF_HIST, :]

    off0 = CONF_HIST - (CONF_KW - 1)

    ng = rg // SUBLANES
    for c in range(D_MODEL // LANES):
        cs = slice(c * LANES, (c + 1) * LANES)
        wts = [dww_ref[k, :, cs] for k in range(CONF_KW)]
        bias = dwb_ref[:, cs]

        for i in range(tt // rg):
            r0 = i * rg
            accs = [jnp.broadcast_to(bias[None], (ng, SUBLANES, LANES)), None]
            for k in range(CONF_KW):
                a, r = divmod(k + off0, SUBLANES)
                xk = sb[r, pl.ds(r0 + a * SUBLANES, rg), cs]
                term = xk.reshape(ng, SUBLANES, LANES) * wts[k][None]
                accs[k % 2] = term if accs[k % 2] is None else accs[k % 2] + term
            cv[pl.ds(r0, rg), cs] = (accs[0] + accs[1]).reshape(rg, LANES)

    def chunk(i, carry):
        r0 = pl.multiple_of(i * rc, rc)
        acc = cv[pl.ds(r0, rc), :]
        mu = jnp.mean(acc, axis=-1, keepdims=True)
        cen = acc - mu
        var = jnp.mean(cen * cen, axis=-1, keepdims=True)
        yn = cen * lax.rsqrt(var + EPS) * lng_ref[...] + lnb_ref[...]
        act[pl.ds(r0, rc), :] = _silu(yn).astype(BF16)
        return carry

    lax.fori_loop(0, tt // rc, chunk, 0)
    sb[0, 0:CONF_HIST, :] = sb[0, tt:tt + CONF_HIST, :]
    r = _dot(act[...], wout_ref[...]) + bout_ref[...]
    out_ref[0] = x_ref[0] + gate_ref[0] * _rms(r, g1_ref[...])


def _conformer(x, g, sc, sh, gate, g1, w_in, b_in, past32, dw_w, dw_b, ln_g, ln_b, w_out, b_out):
    b, t, d = x.shape
    tt = min(t, 256)
    rc = min(tt, 64)
    rg = min(tt, 64)
    row = lambda i, j: (i, j, 0)
    per_b = lambda i, j: (i, 0, 0)
    const2 = lambda i, j: (0, 0)
    vec = lambda v: v.reshape(1, -1)
    dw_w = jnp.broadcast_to(dw_w[:, None, :], (CONF_KW, SUBLANES, d))
    dw_b8 = jnp.broadcast_to(dw_b[None, :], (SUBLANES, d))
    kern = functools.partial(_conf_kernel, tt=tt, rc=rc, rg=rg)
    return pl.pallas_call(
        kern,
        grid=(b, t // tt),
        in_specs=[pl.BlockSpec((1, tt, d), row),
                  pl.BlockSpec((1, d), const2),
                  pl.BlockSpec((1, 1, d), per_b),
                  pl.BlockSpec((1, 1, d), per_b),
                  pl.BlockSpec((1, 1, d), per_b),
                  pl.BlockSpec((1, d), const2),
                  pl.BlockSpec((d, 2 * d), const2),
                  pl.BlockSpec((1, 2 * d), const2),
                  pl.BlockSpec((1, CONF_HIST, d), per_b),
                  pl.BlockSpec((CONF_KW, SUBLANES, d), lambda i, j: (0, 0, 0)),
                  pl.BlockSpec((SUBLANES, d), const2),
                  pl.BlockSpec((1, d), const2),
                  pl.BlockSpec((1, d), const2),
                  pl.BlockSpec((d, d), const2),
                  pl.BlockSpec((1, d), const2)],
        out_specs=[pl.BlockSpec((1, tt, d), row),
                   pl.BlockSpec((1, CONF_KW - 1, d), per_b)],
        out_shape=[jax.ShapeDtypeStruct((b, t, d), F32),
                   jax.ShapeDtypeStruct((b, CONF_KW - 1, d), F32)],
        scratch_shapes=[pltpu.VMEM((SUBLANES, CONF_HIST + tt + SUBLANES, d), F32),
                        pltpu.VMEM((tt, d), BF16),
                        pltpu.VMEM((tt, d), F32)],
        compiler_params=_cparams(("arbitrary", "arbitrary")),
        name="conformer",
    )(x, g, sc, sh, gate, g1, w_in, vec(b_in), past32, dw_w, dw_b8, vec(ln_g), vec(ln_b), w_out,
      vec(b_out))


def _ffn_kernel(x_ref, g2_ref, sc_ref, sh_ref, gate_ref, g3_ref, wg_ref, wu_ref, wd_ref, out_ref):
    h = _norm_mod(x_ref[0], g2_ref[...], sc_ref[0], sh_ref[0]).astype(BF16)
    a = (_silu(_dot(h, wg_ref[...])) * _dot(h, wu_ref[...])).astype(BF16)
    out_ref[0] = x_ref[0] + gate_ref[0] * _rms(_dot(a, wd_ref[...]), g3_ref[...])


def _ffn(x, g2, sc, sh, gate, g3, w_up, w_down):
    b0, t0, d = x.shape
    if b0 * t0 <= 512 and b0 > 1:
        flat = lambda m: jnp.broadcast_to(m, (b0, t0, d)).reshape(1, b0 * t0, d)
        x, sc, sh, gate = flat(x), flat(sc), flat(sh), flat(gate)
    b, t, _ = x.shape
    tm = min(t, 512)
    row = lambda i, j: (i, j, 0)
    const2 = lambda i, j: (0, 0)
    mod_spec = (pl.BlockSpec((1, tm, d), row) if sc.shape[1] == t and t > 1
                else pl.BlockSpec((1, 1, d), lambda i, j: (i, 0, 0)))
    return pl.pallas_call(
        _ffn_kernel,
        grid=(b, t // tm),
        in_specs=[pl.BlockSpec((1, tm, d), row),
                  pl.BlockSpec((1, d), const2),
                  mod_spec,
                  mod_spec,
                  mod_spec,
                  pl.BlockSpec((1, d), const2),
                  pl.BlockSpec((d, D_FF), lambda i, j: (0, 0)),
                  pl.BlockSpec((d, D_FF), lambda i, j: (0, 1)),
                  pl.BlockSpec((D_FF, d), const2)],
        out_specs=pl.BlockSpec((1, tm, d), row),
        out_shape=jax.ShapeDtypeStruct((b, t, d), F32),
        compiler_params=_cparams(("arbitrary", "arbitrary")),
        name="ffn",
    )(x, g2, sc, sh, gate, g3, w_up, w_up, w_down).reshape(b0, t0, d)


def _prep_weights(w):
    hyb = w["hyb_w_in"]
    dt_cols = hyb[:, :, 3 * D_QK + D_INNER + D_XBC:]
    zxd = jnp.concatenate(
        [hyb[:, :, 3 * D_QK:3 * D_QK + D_INNER + D_XBC],
         jnp.pad(dt_cols, ((0, 0), (0, 0), (0, D_DT_PAD - H_SSM)))], axis=-1)
    return dict(
        w_qkv=hyb[:, :, 0:3 * D_QK].astype(BF16),
        w_zxd=zxd.astype(BF16),
        hyb_w_out=w["hyb_w_out"].astype(BF16),
        ffn_w_up=w["ffn_w_up"].astype(BF16),
        ffn_w_down=w["ffn_w_down"].astype(BF16),
        conf_w_in=w["conf_w_in"].astype(BF16),
        conf_w_out=w["conf_w_out"].astype(BF16),
    )


def _trunk(x, mod, past, w, wb):
    b, t, d = x.shape
    ks, vs, sconvs, ssms, cconvs = [], [], [], [], []
    for l in range(DEPTH):
        pieces = [mod[l][:, None, i * d:(i + 1) * d] for i in range(6)]
        shift_m, scale_m, gate_m, shift_f, scale_f, gate_f = pieces
        ng = lambda i: w["norm_g"][l, i].reshape(1, d)
        j = l // 2
        if l % 2 == 0:
            lam_init = 0.8 - 0.6 * math.exp(-0.3 * l)
            qkv, k_rows, v_rows = _qkv_proj(x, ng(0), scale_m, shift_m, wb["w_qkv"][j])
            if past is None:
                o = _attention_seq(qkv, w["attn_lambda"][j], w["attn_subln_g"][j], lam_init,
                                   tq=min(t, 512))
                conv_past = jnp.zeros((b, SSM_CONV - 1, D_XBC), F32)
                ssm_past = jnp.zeros((b, H_SSM * SSM_HEADDIM, D_STATE), F32)
            else:
                pos0 = past[0].shape[2]
                kv_len = pos0 + t
                kv_pad = -(-kv_len // LANES) * LANES
                zpad = jnp.zeros((b, kv_pad - kv_len, D_QK), BF16)
                k_all = jnp.concatenate([past[0][j].reshape(b, pos0, D_QK).astype(BF16),
                                         qkv[:, :, D_QK:2 * D_QK], zpad], axis=1)
                v_all = jnp.concatenate([past[1][j].reshape(b, pos0, D_ATT).astype(BF16),
                                         qkv[:, :, 2 * D_QK:], zpad], axis=1)
                o = _attention(qkv, 0, k_all, 0, v_all, 0, w["attn_lambda"][j],
                               w["attn_subln_g"][j], lam_init, pos0=pos0, kv_len=kv_len, tq=t,
                               tk=kv_pad)
                conv_past = past[2][j]
                ssm_past = past[3][j].reshape(b, H_SSM * SSM_HEADDIM, D_STATE)
            conv_past8 = jnp.pad(conv_past, ((0, 0), (SUBLANES - (SSM_CONV - 1), 0), (0, 0)))
            tl = 256 if t % 256 == 0 else LANES
            t_pad = -(-t // tl) * tl
            x_ssd = x if t_pad == t else jnp.pad(x, ((0, 0), (0, t_pad - t), (0, 0)))
            y, conv_new, ssm_new = _ssd_branch(
                x_ssd, ng(0), scale_m, shift_m, wb["w_zxd"][j], conv_past8, ssm_past,
                w["ssm_conv_w"][j], w["ssm_conv_b"][j], w["ssm_dt_bias"][j], w["ssm_a_log"][j],
                w["ssm_d"][j], w["ssm_norm_g"][j], tl=tl, n_valid=tl if t_pad == t else t)
            y = y[:, :t]
            x = _mix_out(o, y, x, gate_m, ng(1), wb["hyb_w_out"][j])
            ks.append(k_rows.reshape(b, t, H_ATT, 2 * D_HEAD))
            vs.append(v_rows.reshape(b, t, H_ATT, D_V))
            sconvs.append(conv_new)
            ssms.append(ssm_new.reshape(b, H_SSM, SSM_HEADDIM, D_STATE))
        else:
            if past is None:
                past32 = jnp.zeros((b, CONF_HIST, d), F32)
            else:
                past32 = jnp.pad(past[4][j], ((0, 0), (CONF_HIST - (CONF_KW - 1), 0), (0, 0)))
            x, conv_new = _conformer(x, ng(0), scale_m, shift_m, gate_m, ng(1), wb["conf_w_in"][j],
                                     w["conf_b_in"][j], past32, w["conf_dw_w"][j], w["conf_dw_b"][j],
                                     w["conf_ln_g"][j], w["conf_ln_b"][j], wb["conf_w_out"][j],
                                     w["conf_b_out"][j])
            cconvs.append(conv_new)
        x = _ffn(x, ng(2), scale_f, shift_f, gate_f, ng(3), wb["ffn_w_up"][l], wb["ffn_w_down"][l])
    return x, jnp.stack(ks), jnp.stack(vs), jnp.stack(sconvs), jnp.stack(ssms), jnp.stack(cconvs)


def kernel(x_prompt, x_sample, c_prompt, c_sample, cache_attn_k, cache_attn_v, state_ssm_conv, state_ssm, state_conf_conv, ada_w, ada_b, norm_g, ffn_w_up, ffn_w_down, hyb_w_in, attn_lambda, attn_subln_g, ssm_conv_w, ssm_conv_b, ssm_dt_bias, ssm_a_log, ssm_d, ssm_norm_g, hyb_w_out, conf_w_in, conf_b_in, conf_dw_w, conf_dw_b, conf_ln_g, conf_ln_b, conf_w_out, conf_b_out):
    w = dict(norm_g=norm_g, ffn_w_up=ffn_w_up, ffn_w_down=ffn_w_down, hyb_w_in=hyb_w_in,
             attn_lambda=attn_lambda, attn_subln_g=attn_subln_g, ssm_conv_w=ssm_conv_w,
             ssm_conv_b=ssm_conv_b, ssm_dt_bias=ssm_dt_bias, ssm_a_log=ssm_a_log, ssm_d=ssm_d,
             ssm_norm_g=ssm_norm_g, hyb_w_out=hyb_w_out, conf_w_in=conf_w_in, conf_b_in=conf_b_in,
             conf_dw_w=conf_dw_w, conf_dw_b=conf_dw_b, conf_ln_g=conf_ln_g, conf_ln_b=conf_ln_b,
             conf_w_out=conf_w_out, conf_b_out=conf_b_out)
    wb = _prep_weights(w)
    bp, bs = c_prompt.shape[0], c_sample.shape[0]
    rows = -(-(bp + bs) // 16) * 16
    c_all = jnp.concatenate([c_prompt, c_sample, jnp.zeros((rows - bp - bs, D_MODEL), F32)], axis=0)
    mod = _ada(c_all, ada_w, ada_b)
    y_p, k_p, v_p, sconv_p, ssm_p, cconv_p = _trunk(x_prompt, mod[:, 0:bp], None, w, wb)
    past = (cache_attn_k, cache_attn_v, state_ssm_conv, state_ssm, state_conf_conv)
    y_s, k_s, v_s, sconv_s, ssm_s, cconv_s = _trunk(x_sample, mod[:, bp:bp + bs], past, w, wb)
    return (y_p, y_s, k_p, v_p, sconv_p, ssm_p, cconv_p, k_s, v_s, sconv_s, ssm_s, cconv_s)
```

```python
import functools
import math

import ml_dtypes
import numpy as np
import jax
import jax.numpy as jnp
from jax import lax
from jax.experimental import pallas as pl
from jax.experimental.pallas import tpu as pltpu

F32 = jnp.float32
BF16 = jnp.bfloat16

D_MODEL = 1024
DEPTH = 4
CHUNK = 64
CHUNK_SHIFT = CHUNK.bit_length() - 1
assert 1 << CHUNK_SHIFT == CHUNK
EPS = 1e-6
H_ATT = 8
D_HEAD = D_MODEL // H_ATT // 2
D_V = 2 * D_HEAD
D_QK = H_ATT * 2 * D_HEAD
D_ATT = H_ATT * D_V
D_INNER = D_MODEL
SSM_HEADDIM = 64
H_SSM = D_INNER // SSM_HEADDIM
D_STATE = 128
N_GROUPS = 2
SSM_CONV = 4
D_XBC = D_INNER + 2 * N_GROUPS * D_STATE
CONF_KW = 31
D_FF = -(-8 * D_MODEL // (3 * 256)) * 256

LANES = 128
SUBLANES = 8
VMEM_LIMIT = 56 * 1024 * 1024
NEG = -1e30
LOG2E = math.log2(math.e)
QK_SCALE = D_HEAD ** -0.5 * LOG2E
D_DT_PAD = LANES
D_ZXD = D_INNER + D_XBC + D_DT_PAD
CONF_HIST = 32


def _cparams(sem):
    return pltpu.CompilerParams(dimension_semantics=sem, vmem_limit_bytes=VMEM_LIMIT)


def _dot(a, b):
    return jnp.dot(a, b, preferred_element_type=F32)


def _dot_nt(a, b):
    return lax.dot_general(a, b, (((1,), (1,)), ((), ())), preferred_element_type=F32)


def _dot_tn(a, b):
    return lax.dot_general(a, b, (((0,), (0,)), ((), ())), preferred_element_type=F32)


def _rms(xf, g):
    return xf * lax.rsqrt(jnp.mean(xf * xf, axis=-1, keepdims=True) + EPS) * g


def _norm_mod(xf, g, sc, sh):
    return _rms(xf, g) * (1.0 + sc) + sh


def _silu(x):
    return x * jax.nn.sigmoid(x)


def _split3(v):
    hi = v.astype(BF16)
    r = v - hi.astype(F32)
    mid = r.astype(BF16)
    lo = (r - mid.astype(F32)).astype(BF16)
    return hi, mid, lo


def _ada_kernel(c_ref, w_ref, b_ref, o_ref):
    c = c_ref[...]
    s = _silu(c).astype(BF16)
    o_ref[0] = _dot(s, w_ref[0].astype(BF16)) + b_ref[0]


def _ada(c, ada_w, ada_b):
    r = c.shape[0]
    tn = 1536
    return pl.pallas_call(
        _ada_kernel,
        grid=(DEPTH, 6 * D_MODEL // tn),
        in_specs=[pl.BlockSpec((r, D_MODEL), lambda l, j: (0, 0)),
                  pl.BlockSpec((1, D_MODEL, tn), lambda l, j: (l, 0, j)),
                  pl.BlockSpec((1, 1, tn), lambda l, j: (l, 0, j))],
        out_specs=pl.BlockSpec((1, r, tn), lambda l, j: (l, 0, j)),
        out_shape=jax.ShapeDtypeStruct((DEPTH, r, 6 * D_MODEL), F32),
        compiler_params=_cparams(("arbitrary", "arbitrary")),
        name="ada",
    )(c, ada_w, ada_b.reshape(DEPTH, 1, 6 * D_MODEL))


def _qkv_kernel(x_ref, g_ref, sc_ref, sh_ref, w_ref, qkv_ref, k_ref, v_ref):
    h = _norm_mod(x_ref[0], g_ref[...], sc_ref[0], sh_ref[0]).astype(BF16)
    r = _dot(h, w_ref[...])
    qkv_ref[0, :, 0:D_QK] = (r[:, 0:D_QK] * QK_SCALE).astype(BF16)
    qkv_ref[0, :, D_QK:] = r[:, D_QK:].astype(BF16)
    k_ref[0] = r[:, D_QK:2 * D_QK]
    v_ref[0] = r[:, 2 * D_QK:]


def _qkv_proj(x, g, sc, sh, w):
    b, t, d = x.shape
    tm = min(t, 512)
    n = w.shape[1]
    row = lambda i, j: (i, j, 0)
    per_b = lambda i, j: (i, 0, 0)
    return pl.pallas_call(
        _qkv_kernel,
        grid=(b, t // tm),
        in_specs=[pl.BlockSpec((1, tm, d), row),
                  pl.BlockSpec((1, d), lambda i, j: (0, 0)),
                  pl.BlockSpec((1, 1, d), per_b),
                  pl.BlockSpec((1, 1, d), per_b),
                  pl.BlockSpec((d, n), lambda i, j: (0, 0))],
        out_specs=[pl.BlockSpec((1, tm, n), row),
                   pl.BlockSpec((1, tm, D_QK), row),
                   pl.BlockSpec((1, tm, D_ATT), row)],
        out_shape=[jax.ShapeDtypeStruct((b, t, n), BF16),
                   jax.ShapeDtypeStruct((b, t, D_QK), F32),
                   jax.ShapeDtypeStruct((b, t, D_ATT), F32)],
        compiler_params=_cparams(("arbitrary", "arbitrary")),
        name="qkv_proj",
    )(x, g, sc, sh, w)


def _attn_kernel(slopes_ref, lam_ref, subg_ref, q_ref, k_ref, v_ref, o_ref, m_sc, l_sc, acc_sc,
                 *, tq, tk, pos0, kv_len, lam_init, nk):
    hd = pl.program_id(1)
    qi = pl.program_id(2)
    ki = pl.program_id(3)
    slope = slopes_ref[hd]
    qs = pos0 + qi * tq
    ks = ki * tk
    last_chunk_end = ((qs + tq - 1) // CHUNK) * CHUNK + CHUNK - 1
    last_needed = jnp.minimum(nk - 1, last_chunk_end // tk)

    @pl.when(ki == 0)
    def _():
        m_sc[...] = jnp.full(m_sc.shape, NEG, F32)
        l_sc[...] = jnp.zeros(l_sc.shape, F32)
        acc_sc[...] = jnp.zeros(acc_sc.shape, F32)

    def update(bias, mask):
        q = q_ref[0]
        k = k_ref[0]
        v = v_ref[0]
        for m in range(2):
            s = (_dot_nt(q[:, m * D_HEAD:(m + 1) * D_HEAD], k[:, m * D_HEAD:(m + 1) * D_HEAD])
                 + LOG2E * bias)
            if mask is not None:
                s = jnp.where(mask, s, NEG)
            m_old = m_sc[m]
            m_new = jnp.maximum(m_old, jnp.max(s, axis=-1, keepdims=True))
            alpha = jnp.exp2(m_old - m_new)
            p = jnp.exp2(s - m_new)
            l_sc[m] = alpha * l_sc[m] + jnp.sum(p, axis=-1, keepdims=True)
            acc_sc[m] = alpha * acc_sc[m] + _dot(p.astype(BF16), v)
            m_sc[m] = m_new

    interior = jnp.logical_and(ks + tk - 1 <= qs, ks + tk <= kv_len)

    @pl.when(jnp.logical_and(ki <= last_needed, interior))
    def _():
        col = lax.broadcasted_iota(jnp.int32, (1, tk), 1) + (ks - qs)
        update(slope * col.astype(F32), None)

    @pl.when(jnp.logical_and(ki <= last_needed, jnp.logical_not(interior)))
    def _():
        row = lax.broadcasted_iota(jnp.int32, (tq, tk), 0)
        qpos = row + qs
        kpos = lax.broadcasted_iota(jnp.int32, (tq, tk), 1) + ks
        bias = slope * (row - jnp.abs(qpos - kpos)).astype(F32)
        mask = jnp.logical_and(jnp.right_shift(kpos, CHUNK_SHIFT) <= jnp.right_shift(qpos, CHUNK_SHIFT),
                               kpos < kv_len)
        update(bias, mask)

    @pl.when(ki == nk - 1)
    def _():
        lv = lam_ref[...]
        lam = (jnp.exp(jnp.sum(lv[0:1] * lv[1:2], axis=-1, keepdims=True))
               - jnp.exp(jnp.sum(lv[2:3] * lv[3:4], axis=-1, keepdims=True)) + lam_init)
        o = acc_sc[0] / l_sc[0] - lam * (acc_sc[1] / l_sc[1])
        o = _rms(o, subg_ref[...]) * (1.0 - lam_init)
        o_ref[0] = o.astype(o_ref.dtype)


def _attention(q_arr, q_col0, k_arr, k_col0, v_arr, v_col0, lam_vec, subln_g, lam_init,
               *, pos0, kv_len, tq, tk):
    b, t_q = q_arr.shape[0], q_arr.shape[1]
    t_k = k_arr.shape[1]
    nq, nk = t_q // tq, t_k // tk
    slopes = _alibi_slopes()

    def kv_block(qi, ki):
        last = ((pos0 + qi * tq + tq - 1) // CHUNK * CHUNK + CHUNK - 1) // tk
        return jnp.minimum(ki, jnp.minimum(last, nk - 1))

    kern = functools.partial(_attn_kernel, tq=tq, tk=tk, pos0=pos0, kv_len=kv_len,
                             lam_init=lam_init, nk=nk)
    return pl.pallas_call(
        kern,
        grid=(b, H_ATT, nq, nk),
        in_specs=[pl.BlockSpec(memory_space=pltpu.SMEM),
                  pl.BlockSpec((4, D_HEAD), lambda bb, h, qi, ki: (0, 0)),
                  pl.BlockSpec((1, D_V), lambda bb, h, qi, ki: (0, 0)),
                  pl.BlockSpec((1, tq, D_V), lambda bb, h, qi, ki: (bb, qi, q_col0 + h)),
                  pl.BlockSpec((1, tk, D_V), lambda bb, h, qi, ki: (bb, kv_block(qi, ki), k_col0 + h)),
                  pl.BlockSpec((1, tk, D_V), lambda bb, h, qi, ki: (bb, kv_block(qi, ki), v_col0 + h))],
        out_specs=pl.BlockSpec((1, tq, D_V), lambda bb, h, qi, ki: (bb, qi, h)),
        out_shape=jax.ShapeDtypeStruct((b, t_q, D_ATT), BF16),
        scratch_shapes=[pltpu.VMEM((2, tq, 1), F32), pltpu.VMEM((2, tq, 1), F32),
                        pltpu.VMEM((2, tq, D_V), F32)],
        compiler_params=_cparams(("arbitrary", "arbitrary", "arbitrary", "arbitrary")),
        name="diff_attn",
    )(slopes, lam_vec, subln_g.reshape(1, D_V), q_arr, k_arr, v_arr)


N_AUG = 3
AUG_W = 2 * N_AUG
V_AUG_W = 2 * D_V


def _alibi_slopes():
    return 2.0 ** (-8.0 * jnp.arange(1, H_ATT + 1, dtype=F32) / H_ATT)


def _alibi_tables(t):
    slopes = np.float32(2.0) ** (np.float32(-8.0) * np.arange(1, H_ATT + 1, dtype=np.float32) / H_ATT)
    bias = (slopes * np.float32(LOG2E))[:, None] * np.arange(t, dtype=np.float32)[None, :]

    def split(v):
        terms = []
        for _ in range(N_AUG):
            part = v.astype(ml_dtypes.bfloat16)
            terms.append(part)
            v = v - part.astype(np.float32)
        return terms

    ones = [np.ones(bias.shape, ml_dtypes.bfloat16)] * N_AUG
    k_terms = jnp.asarray(np.stack(split(bias) + ones, axis=-1))
    q_terms = jnp.asarray(np.stack(ones + split(-bias), axis=-1))

    def spread(terms):
        halves = [jnp.pad(terms, ((0, 0), (0, 0), (base, LANES - base - AUG_W))) for base in (D_HEAD, 0)]
        return jnp.stack(halves, axis=1)

    return spread(k_terms), spread(q_terms)


def _fill_features(lane, half, feat, table):
    keep = (lane < D_HEAD) if half == 0 else (lane >= D_HEAD)
    return jnp.where(keep, feat.astype(F32), table.astype(F32)).astype(BF16)


def _attn_seq_kernel(slopes_ref, lam_ref, subg_ref, q_ref, k_ref, v_ref, kt_ref, qt_ref, o_ref,
                     ka_sc, va_sc, qa_sc, corr_sc, s_buf, m_sc, mslot_sc, alpha_sc, acc_sc,
                     *, tq, t, lam_init):
    hd = pl.program_id(1)
    qi = pl.program_id(2)
    rows_b = 512
    unroll = 2

    @pl.when(qi == 0)
    def _():
        def build(c, carry):
            r0 = pl.multiple_of(c * rows_b, rows_b)
            lane = lax.broadcasted_iota(jnp.int32, (rows_b, LANES), 1)
            kk = k_ref[0, pl.ds(r0, rows_b), :]
            for half in range(2):
                ka_sc[half, pl.ds(r0, rows_b), :] = _fill_features(
                    lane, half, kk, kt_ref[0, half, pl.ds(r0, rows_b), :])
            va_sc[pl.ds(r0, rows_b), 0:D_V] = v_ref[0, pl.ds(r0, rows_b), :]
            va_sc[pl.ds(r0, rows_b), D_V:] = jnp.where(lane == 0, 1.0, 0.0).astype(BF16)
            return carry
        lax.fori_loop(0, t // rows_b, build, 0)
        ri = lax.broadcasted_iota(jnp.int32, (tq, tq), 0)
        cj = lax.broadcasted_iota(jnp.int32, (tq, tq), 1)
        fix = (-2.0 * LOG2E * slopes_ref[hd]) * jnp.maximum(cj - ri, 0).astype(F32)
        vis = jnp.right_shift(cj, CHUNK_SHIFT) <= jnp.right_shift(ri, CHUNK_SHIFT)
        corr_sc[...] = jnp.where(vis, fix, NEG)

    lane_q = lax.broadcasted_iota(jnp.int32, (tq, LANES), 1)
    qq = q_ref[0]
    for half in range(2):
        qa_sc[half] = _fill_features(lane_q, half, qq, qt_ref[0, half])
    m_sc[...] = jnp.full(m_sc.shape, NEG, F32)
    acc_sc[...] = jnp.zeros(acc_sc.shape, F32)

    def stage_a(j, slot, diag):
        k0 = pl.multiple_of(j * tq, tq)
        for half in range(2):
            s = _dot_nt(qa_sc[half], ka_sc[half, pl.ds(k0, tq), :])
            if diag:
                s = s + corr_sc[...]
            s_buf[slot, half] = s
            mx = s[:, 0:LANES]
            for c in range(1, tq // LANES):
                mx = jnp.maximum(mx, s[:, c * LANES:(c + 1) * LANES])
            m_old = m_sc[half]
            m_new = jnp.maximum(m_old, jnp.max(mx, axis=-1, keepdims=True))
            alpha_sc[slot, half] = jnp.exp2(m_old - m_new)
            mslot_sc[slot, half] = m_new
            m_sc[half] = m_new

    def stage_b(j, slot):
        k0 = pl.multiple_of(j * tq, tq)
        vblk = va_sc[pl.ds(k0, tq), :]
        for half in range(2):
            mn = mslot_sc[slot, half]
            p = jnp.concatenate(
                [jnp.exp2(s_buf[slot, half, :, c * LANES:(c + 1) * LANES] - mn).astype(BF16)
                 for c in range(tq // LANES)], axis=-1)
            al = alpha_sc[slot, half]
            acc_sc[half] = jnp.concatenate([al, al], axis=-1) * acc_sc[half] + _dot(p, vblk)

    @pl.when(qi == 0)
    def _():
        stage_a(0, 0, True)
        stage_b(0, 0)

    @pl.when(qi > 0)
    def _():
        stage_a(0, 0, False)

        n_it = qi - 1
        n_main = n_it // unroll

        def body_main(i, carry):
            for u in range(unroll):
                j = i * unroll + u
                stage_a(j + 1, (u + 1) % 2, False)
                stage_b(j, u % 2)
            return carry

        def body_rest(j, carry):
            stage_a(j + 1, (j + 1) % 2, False)
            stage_b(j, j % 2)
            return carry
        lax.fori_loop(0, n_main, body_main, 0)
        lax.fori_loop(n_main * unroll, n_it, body_rest, 0)
        stage_a(qi, qi % 2, True)
        stage_b(qi - 1, (qi - 1) % 2)
        stage_b(qi, qi % 2)

    lv = lam_ref[...]
    lam = (jnp.exp(jnp.sum(lv[0:1] * lv[1:2], axis=-1, keepdims=True))
           - jnp.exp(jnp.sum(lv[2:3] * lv[3:4], axis=-1, keepdims=True)) + lam_init)
    o = (acc_sc[0, :, 0:D_V] / acc_sc[0, :, D_V:D_V + 1]
         - lam * (acc_sc[1, :, 0:D_V] / acc_sc[1, :, D_V:D_V + 1]))
    o_ref[0] = (_rms(o, subg_ref[...]) * (1.0 - lam_init)).astype(o_ref.dtype)


def _attention_seq(qkv, lam_vec, subln_g, lam_init, *, tq):
    b, t = qkv.shape[0], qkv.shape[1]
    assert t % tq == 0 and tq % CHUNK == 0 and t % 512 == 0
    kt, qt = _alibi_tables(t)
    kern = functools.partial(_attn_seq_kernel, tq=tq, t=t, lam_init=lam_init)
    return pl.pallas_call(
        kern,
        grid=(b, H_ATT, t // tq),
        in_specs=[pl.BlockSpec(memory_space=pltpu.SMEM),
                  pl.BlockSpec((4, D_HEAD), lambda bb, h, qi: (0, 0)),
                  pl.BlockSpec((1, D_V), lambda bb, h, qi: (0, 0)),
                  pl.BlockSpec((1, tq, D_V), lambda bb, h, qi: (bb, qi, h)),
                  pl.BlockSpec((1, t, D_V), lambda bb, h, qi: (bb, 0, H_ATT + h)),
                  pl.BlockSpec((1, t, D_V), lambda bb, h, qi: (bb, 0, 2 * H_ATT + h)),
                  pl.BlockSpec((1, 2, t, LANES), lambda bb, h, qi: (h, 0, 0, 0)),
                  pl.BlockSpec((1, 2, tq, LANES), lambda bb, h, qi: (h, 0, qi, 0))],
        out_specs=pl.BlockSpec((1, tq, D_V), lambda bb, h, qi: (bb, qi, h)),
        out_shape=jax.ShapeDtypeStruct((b, t, D_ATT), BF16),
        scratch_shapes=[pltpu.VMEM((2, t, LANES), BF16),
                        pltpu.VMEM((t, V_AUG_W), BF16),
                        pltpu.VMEM((2, tq, LANES), BF16),
                        pltpu.VMEM((tq, tq), F32),
                        pltpu.VMEM((2, 2, tq, tq), F32),
                        pltpu.VMEM((2, tq, LANES), F32),
                        pltpu.VMEM((2, 2, tq, LANES), F32),
                        pltpu.VMEM((2, 2, tq, LANES), F32),
                        pltpu.VMEM((2, tq, V_AUG_W), F32)],
        compiler_params=_cparams(("arbitrary", "arbitrary", "arbitrary")),
        name="diff_attn_seq",
    )(_alibi_slopes(), lam_vec, subln_g.reshape(1, D_V), qkv, qkv, qkv, kt, qt)


def _ssd_kernel(x_ref, g_ref, sc_ref, sh_ref, w_ref, convp_ref, ssmp_ref, cw_ref, cb_ref, dtb_ref,
                alog_ref, dskip_ref, ng_ref, tri_ref, e_ref,
                y_ref, convo_ref, ssmo_ref, cbuf, state, ybuf, *, tl, n_valid):
    t = pl.program_id(1)

    @pl.when(t == 0)
    def _():
        cbuf[0:SUBLANES, :] = convp_ref[0]
        state[...] = ssmp_ref[0]

    h = _norm_mod(x_ref[0], g_ref[...], sc_ref[0], sh_ref[0]).astype(BF16)
    zxd = _dot(h, w_ref[...])
    z = zxd[:, 0:D_INNER]
    raw = zxd[:, D_INNER:D_INNER + D_XBC]
    dt_raw = zxd[:, D_INNER + D_XBC:]

    cbuf[SUBLANES:SUBLANES + tl, :] = raw
    conv = raw * cw_ref[SSM_CONV - 1:SSM_CONV, :] + cb_ref[...]
    for j in range(1, SSM_CONV):
        conv = conv + cbuf[SUBLANES - j:SUBLANES - j + tl, :] * cw_ref[SSM_CONV - 1 - j:SSM_CONV - j, :]
    convo_ref[0] = cbuf[SUBLANES + n_valid - (SSM_CONV - 1):SUBLANES + n_valid, :]
    cbuf[0:SUBLANES, :] = cbuf[tl:tl + SUBLANES, :]
    xbc = _silu(conv)
    xs = xbc[:, 0:D_INNER]
    bmat = xbc[:, D_INNER:D_INNER + N_GROUPS * D_STATE].astype(BF16)
    cmat = xbc[:, D_INNER + N_GROUPS * D_STATE:].astype(BF16)

    u = dt_raw + dtb_ref[...]
    dt = jnp.maximum(u, 0.0) + jnp.log1p(jnp.exp(-jnp.abs(u)))
    if n_valid < tl:
        rows = lax.broadcasted_iota(jnp.int32, dt.shape, 0)
        dt = jnp.where(rows < n_valid, dt, 0.0)
    da = dt * (-jnp.exp(alog_ref[...]))
    tri = tri_ref[...]
    acum = sum(_dot(tri, p) for p in _split3(da))
    acum_t = acum.T
    e = e_ref[...]
    dt_e = sum(_dot(p, e) for p in _split3(dt)[:2])
    acum_e = sum(_dot(p, e) for p in _split3(acum))
    xdt = xs * dt_e
    xdt_b = xdt.astype(BF16)
    last_e = acum_e[n_valid - 1:n_valid, :]
    xdec_b = (xdt * jnp.exp(last_e - acum_e)).astype(BF16)

    st = state[...]
    st_b = st.astype(BF16)
    causal = (lax.broadcasted_iota(jnp.int32, (tl, tl), 0) >= lax.broadcasted_iota(jnp.int32, (tl, tl), 1))
    lane = lax.broadcasted_iota(jnp.int32, (tl, LANES), 1)
    hpg = H_SSM // N_GROUPS
    gw = hpg * SSM_HEADDIM
    for g in range(N_GROUPS):
        bg = bmat[:, g * D_STATE:(g + 1) * D_STATE]
        cg = cmat[:, g * D_STATE:(g + 1) * D_STATE]
        cb = _dot_nt(cg, bg)
        ybuf[:, g * gw:(g + 1) * gw] = _dot_nt(cg, st_b[g * gw:(g + 1) * gw, :])
        for pr in range(hpg // 2):
            c0 = g * gw + pr * LANES
            x2 = xdt_b[:, c0:c0 + LANES]
            acc = None
            for half in range(2):
                hh = g * hpg + pr * 2 + half
                seg = acum[:, hh:hh + 1] - acum_t[hh:hh + 1, :]
                mm = (jnp.exp(jnp.where(causal, seg, NEG)) * cb).astype(BF16)
                keep = (lane < SSM_HEADDIM) if half == 0 else (lane >= SSM_HEADDIM)
                part = _dot(mm, jnp.where(keep, x2, jnp.zeros_like(x2)))
                acc = part if acc is None else acc + part
            ybuf[:, c0:c0 + LANES] = ybuf[:, c0:c0 + LANES] * jnp.exp(acum_e[:, c0:c0 + LANES]) + acc
        state[g * gw:(g + 1) * gw, :] = _dot_tn(xdec_b[:, g * gw:(g + 1) * gw], bg)

    fb = jnp.broadcast_to(jnp.exp(acum_t[:, n_valid - 1:n_valid]), (LANES, D_STATE))
    for hh in range(H_SSM):
        rows = slice(hh * SSM_HEADDIM, (hh + 1) * SSM_HEADDIM)
        new_rows = st[rows, :] * fb[hh:hh + 1, :] + state[rows, :]
        state[rows, :] = new_rows
        ssmo_ref[0, rows, :] = new_rows

    y = ybuf[...] + dskip_ref[...] * xs
    y = y * _silu(z)
    gn = D_INNER // N_GROUPS
    for g in range(N_GROUPS):
        y_ref[0, :, g * gn:(g + 1) * gn] = _rms(y[:, g * gn:(g + 1) * gn],
                                                ng_ref[:, g * gn:(g + 1) * gn]).astype(y_ref.dtype)


def _ssd_branch(x, g, sc, sh, w_zxd, conv_past8, ssm_past, conv_w, conv_b, dt_bias, a_log, d_skip,
                norm_g, *, tl, n_valid):
    b, t, d = x.shape
    nt = t // tl
    assert n_valid == tl or nt == 1
    hp = H_SSM * SSM_HEADDIM
    tri = jnp.asarray(np.tril(np.ones((tl, tl), np.float32)), BF16)
    e_np = np.zeros((LANES, D_INNER), np.float32)
    for hh in range(H_SSM):
        e_np[hh, hh * SSM_HEADDIM:(hh + 1) * SSM_HEADDIM] = 1.0
    e = jnp.asarray(e_np, BF16)
    pad = lambda v: jnp.pad(v.astype(F32), (0, LANES - H_SSM)).reshape(1, LANES)
    row = lambda i, j: (i, j, 0)
    per_b = lambda i, j: (i, 0, 0)
    const2 = lambda i, j: (0, 0)
    kern = functools.partial(_ssd_kernel, tl=tl, n_valid=n_valid)
    return pl.pallas_call(
        kern,
        grid=(b, nt),
        in_specs=[pl.BlockSpec((1, tl, d), row),
                  pl.BlockSpec((1, d), const2),
                  pl.BlockSpec((1, 1, d), per_b),
                  pl.BlockSpec((1, 1, d), per_b),
                  pl.BlockSpec((d, D_ZXD), const2),
                  pl.BlockSpec((1, SUBLANES, D_XBC), per_b),
                  pl.BlockSpec((1, hp, D_STATE), per_b),
                  pl.BlockSpec((SSM_CONV, D_XBC), const2),
                  pl.BlockSpec((1, D_XBC), const2),
                  pl.BlockSpec((1, LANES), const2),
                  pl.BlockSpec((1, LANES), const2),
                  pl.BlockSpec((1, D_INNER), const2),
                  pl.BlockSpec((1, D_INNER), const2),
                  pl.BlockSpec((tl, tl), const2),
                  pl.BlockSpec((LANES, D_INNER), const2)],
        out_specs=[pl.BlockSpec((1, tl, D_INNER), row),
                   pl.BlockSpec((1, SSM_CONV - 1, D_XBC), per_b),
                   pl.BlockSpec((1, hp, D_STATE), per_b)],
        out_shape=[jax.ShapeDtypeStruct((b, t, D_INNER), BF16),
                   jax.ShapeDtypeStruct((b, SSM_CONV - 1, D_XBC), F32),
                   jax.ShapeDtypeStruct((b, hp, D_STATE), F32)],
        scratch_shapes=[pltpu.VMEM((SUBLANES + tl, D_XBC), F32),
                        pltpu.VMEM((hp, D_STATE), F32),
                        pltpu.VMEM((tl, D_INNER), F32)],
        compiler_params=_cparams(("arbitrary", "arbitrary")),
        name="ssd_branch",
    )(x, g, sc, sh, w_zxd, conv_past8, ssm_past, conv_w, conv_b.reshape(1, D_XBC), pad(dt_bias),
      pad(a_log), jnp.repeat(d_skip.astype(F32), SSM_HEADDIM).reshape(1, D_INNER),
      norm_g.reshape(1, D_INNER), tri, e)


def _mixout_kernel(o_ref, y_ref, x_ref, gate_ref, g1_ref, w_ref, out_ref):
    r = _dot(o_ref[0], w_ref[0:D_ATT, :]) + _dot(y_ref[0], w_ref[D_ATT:, :])
    out_ref[0] = x_ref[0] + gate_ref[0] * _rms(r, g1_ref[...])


def _mix_out(o, y, x, gate, g1, w):
    b, t, d = x.shape
    tm = min(t, 512)
    row = lambda i, j: (i, j, 0)
    per_b = lambda i, j: (i, 0, 0)
    return pl.pallas_call(
        _mixout_kernel,
        grid=(b, t // tm),
        in_specs=[pl.BlockSpec((1, tm, D_ATT), row),
                  pl.BlockSpec((1, tm, D_INNER), row),
                  pl.BlockSpec((1, tm, d), row),
                  pl.BlockSpec((1, 1, d), per_b),
                  pl.BlockSpec((1, d), lambda i, j: (0, 0)),
                  pl.BlockSpec((D_ATT + D_INNER, d), lambda i, j: (0, 0))],
        out_specs=pl.BlockSpec((1, tm, d), row),
        out_shape=jax.ShapeDtypeStruct((b, t, d), F32),
        compiler_params=_cparams(("arbitrary", "arbitrary")),
        name="mix_out",
    )(o, y, x, gate, g1, w)


def _conf_kernel(x_ref, g_ref, sc_ref, sh_ref, gate_ref, g1_ref, win_ref, bin_ref, past_ref, dww_ref,
                 dwb_ref, lng_ref, lnb_ref, wout_ref, bout_ref, out_ref, convo_ref, sb, act, cv,
                 *, tt, rc, rg):
    t = pl.program_id(1)

    @pl.when(t == 0)
    def _():
        sb[0, 0:CONF_HIST, :] = past_ref[0]

    h = _norm_mod(x_ref[0], g_ref[...], sc_ref[0], sh_ref[0]).astype(BF16)
    lin = _dot(h, win_ref[...]) + bin_ref[...]
    u = lin[:, 0:D_MODEL] * jax.nn.sigmoid(lin[:, D_MODEL:])
    sb[0, CONF_HIST:CONF_HIST + tt, :] = u
    span = tt + CONF_HIST - SUBLANES
    for r in range(1, SUBLANES):
        sb[r, 0:span, :] = sb[0, r:r + span, :]
    convo_ref[0] = sb[0, tt + CONF_HIST - (CONF_KW - 1):tt + CONF_HIST, :]

    off0 = CONF_HIST - (CONF_KW - 1)

    ng = rg // SUBLANES
    for c in range(D_MODEL // LANES):
        cs = slice(c * LANES, (c + 1) * LANES)
        wts = [dww_ref[k, :, cs] for k in range(CONF_KW)]
        bias = dwb_ref[:, cs]

        for i in range(tt // rg):
            r0 = i * rg
            accs = [jnp.broadcast_to(bias[None], (ng, SUBLANES, LANES)), None]
            for k in range(CONF_KW):
                a, r = divmod(k + off0, SUBLANES)
                xk = sb[r, pl.ds(r0 + a * SUBLANES, rg), cs]
                term = xk.reshape(ng, SUBLANES, LANES) * wts[k][None]
                accs[k % 2] = term if accs[k % 2] is None else accs[k % 2] + term
            cv[pl.ds(r0, rg), cs] = (accs[0] + accs[1]).reshape(rg, LANES)

    for i in range(tt // rc):
        acc = cv[i * rc:(i + 1) * rc, :]
        mu = jnp.mean(acc, axis=-1, keepdims=True)
        cen = acc - mu
        var = jnp.mean(cen * cen, axis=-1, keepdims=True)
        yn = cen * lax.rsqrt(var + EPS) * lng_ref[...] + lnb_ref[...]
        act[i * rc:(i + 1) * rc, :] = _silu(yn).astype(BF16)
    sb[0, 0:CONF_HIST, :] = sb[0, tt:tt + CONF_HIST, :]
    r = _dot(act[...], wout_ref[...]) + bout_ref[...]
    out_ref[0] = x_ref[0] + gate_ref[0] * _rms(r, g1_ref[...])


def _conformer(x, g, sc, sh, gate, g1, w_in, b_in, past32, dw_w, dw_b, ln_g, ln_b, w_out, b_out):
    b, t, d = x.shape
    tt = min(t, 512)
    rc = min(tt, 64)
    rg = min(tt, 64)
    row = lambda i, j: (i, j, 0)
    per_b = lambda i, j: (i, 0, 0)
    const2 = lambda i, j: (0, 0)
    vec = lambda v: v.reshape(1, -1)
    dw_w = jnp.broadcast_to(dw_w[:, None, :], (CONF_KW, SUBLANES, d))
    dw_b8 = jnp.broadcast_to(dw_b[None, :], (SUBLANES, d))
    kern = functools.partial(_conf_kernel, tt=tt, rc=rc, rg=rg)
    return pl.pallas_call(
        kern,
        grid=(b, t // tt),
        in_specs=[pl.BlockSpec((1, tt, d), row),
                  pl.BlockSpec((1, d), const2),
                  pl.BlockSpec((1, 1, d), per_b),
                  pl.BlockSpec((1, 1, d), per_b),
                  pl.BlockSpec((1, 1, d), per_b),
                  pl.BlockSpec((1, d), const2),
                  pl.BlockSpec((d, 2 * d), const2),
                  pl.BlockSpec((1, 2 * d), const2),
                  pl.BlockSpec((1, CONF_HIST, d), per_b),
                  pl.BlockSpec((CONF_KW, SUBLANES, d), lambda i, j: (0, 0, 0)),
                  pl.BlockSpec((SUBLANES, d), const2),
                  pl.BlockSpec((1, d), const2),
                  pl.BlockSpec((1, d), const2),
                  pl.BlockSpec((d, d), const2),
                  pl.BlockSpec((1, d), const2)],
        out_specs=[pl.BlockSpec((1, tt, d), row),
                   pl.BlockSpec((1, CONF_KW - 1, d), per_b)],
        out_shape=[jax.ShapeDtypeStruct((b, t, d), F32),
                   jax.ShapeDtypeStruct((b, CONF_KW - 1, d), F32)],
        scratch_shapes=[pltpu.VMEM((SUBLANES, CONF_HIST + tt + SUBLANES, d), F32),
                        pltpu.VMEM((tt, d), BF16),
                        pltpu.VMEM((tt, d), F32)],
        compiler_params=_cparams(("arbitrary", "arbitrary")),
        name="conformer",
    )(x, g, sc, sh, gate, g1, w_in, vec(b_in), past32, dw_w, dw_b8, vec(ln_g), vec(ln_b), w_out,
      vec(b_out))


def _ffn_kernel(x_ref, g2_ref, sc_ref, sh_ref, gate_ref, g3_ref, wg_ref, wu_ref, wd_ref, out_ref):
    h = _norm_mod(x_ref[0], g2_ref[...], sc_ref[0], sh_ref[0]).astype(BF16)
    a = (_silu(_dot(h, wg_ref[...])) * _dot(h, wu_ref[...])).astype(BF16)
    out_ref[0] = x_ref[0] + gate_ref[0] * _rms(_dot(a, wd_ref[...]), g3_ref[...])


def _ffn(x, g2, sc, sh, gate, g3, w_up, w_down):
    b0, t0, d = x.shape
    if b0 * t0 <= 512 and b0 > 1:
        flat = lambda m: jnp.broadcast_to(m, (b0, t0, d)).reshape(1, b0 * t0, d)
        x, sc, sh, gate = flat(x), flat(sc), flat(sh), flat(gate)
    b, t, _ = x.shape
    tm = min(t, 512)
    row = lambda i, j: (i, j, 0)
    const2 = lambda i, j: (0, 0)
    mod_spec = (pl.BlockSpec((1, tm, d), row) if sc.shape[1] == t and t > 1
                else pl.BlockSpec((1, 1, d), lambda i, j: (i, 0, 0)))
    return pl.pallas_call(
        _ffn_kernel,
        grid=(b, t // tm),
        in_specs=[pl.BlockSpec((1, tm, d), row),
                  pl.BlockSpec((1, d), const2),
                  mod_spec,
                  mod_spec,
                  mod_spec,
                  pl.BlockSpec((1, d), const2),
                  pl.BlockSpec((d, D_FF), lambda i, j: (0, 0)),
                  pl.BlockSpec((d, D_FF), lambda i, j: (0, 1)),
                  pl.BlockSpec((D_FF, d), const2)],
        out_specs=pl.BlockSpec((1, tm, d), row),
        out_shape=jax.ShapeDtypeStruct((b, t, d), F32),
        compiler_params=_cparams(("arbitrary", "arbitrary")),
        name="ffn",
    )(x, g2, sc, sh, gate, g3, w_up, w_up, w_down).reshape(b0, t0, d)


def _prep_weights(w):
    hyb = w["hyb_w_in"]
    dt_cols = hyb[:, :, 3 * D_QK + D_INNER + D_XBC:]
    zxd = jnp.concatenate(
        [hyb[:, :, 3 * D_QK:3 * D_QK + D_INNER + D_XBC],
         jnp.pad(dt_cols, ((0, 0), (0, 0), (0, D_DT_PAD - H_SSM)))], axis=-1)
    return dict(
        w_qkv=hyb[:, :, 0:3 * D_QK].astype(BF16),
        w_zxd=zxd.astype(BF16),
        hyb_w_out=w["hyb_w_out"].astype(BF16),
        ffn_w_up=w["ffn_w_up"].astype(BF16),
        ffn_w_down=w["ffn_w_down"].astype(BF16),
        conf_w_in=w["conf_w_in"].astype(BF16),
        conf_w_out=w["conf_w_out"].astype(BF16),
    )


def _trunk(x, mod, past, w, wb):
    b, t, d = x.shape
    ks, vs, sconvs, ssms, cconvs = [], [], [], [], []
    for l in range(DEPTH):
        pieces = [mod[l][:, None, i * d:(i + 1) * d] for i in range(6)]
        shift_m, scale_m, gate_m, shift_f, scale_f, gate_f = pieces
        ng = lambda i: w["norm_g"][l, i].reshape(1, d)
        j = l // 2
        if l % 2 == 0:
            lam_init = 0.8 - 0.6 * math.exp(-0.3 * l)
            qkv, k_rows, v_rows = _qkv_proj(x, ng(0), scale_m, shift_m, wb["w_qkv"][j])
            if past is None:
                o = _attention_seq(qkv, w["attn_lambda"][j], w["attn_subln_g"][j], lam_init,
                                   tq=min(t, 512))
                conv_past = jnp.zeros((b, SSM_CONV - 1, D_XBC), F32)
                ssm_past = jnp.zeros((b, H_SSM * SSM_HEADDIM, D_STATE), F32)
            else:
                pos0 = past[0].shape[2]
                kv_len = pos0 + t
                kv_pad = -(-kv_len // LANES) * LANES
                zpad = jnp.zeros((b, kv_pad - kv_len, D_QK), BF16)
                k_all = jnp.concatenate([past[0][j].reshape(b, pos0, D_QK).astype(BF16),
                                         qkv[:, :, D_QK:2 * D_QK], zpad], axis=1)
                v_all = jnp.concatenate([past[1][j].reshape(b, pos0, D_ATT).astype(BF16),
                                         qkv[:, :, 2 * D_QK:], zpad], axis=1)
                o = _attention(qkv, 0, k_all, 0, v_all, 0, w["attn_lambda"][j],
                               w["attn_subln_g"][j], lam_init, pos0=pos0, kv_len=kv_len, tq=t,
                               tk=kv_pad)
                conv_past = past[2][j]
                ssm_past = past[3][j].reshape(b, H_SSM * SSM_HEADDIM, D_STATE)
            conv_past8 = jnp.pad(conv_past, ((0, 0), (SUBLANES - (SSM_CONV - 1), 0), (0, 0)))
            tl = 256 if t % 256 == 0 else LANES
            t_pad = -(-t // tl) * tl
            x_ssd = x if t_pad == t else jnp.pad(x, ((0, 0), (0, t_pad - t), (0, 0)))
            y, conv_new, ssm_new = _ssd_branch(
                x_ssd, ng(0), scale_m, shift_m, wb["w_zxd"][j], conv_past8, ssm_past,
                w["ssm_conv_w"][j], w["ssm_conv_b"][j], w["ssm_dt_bias"][j], w["ssm_a_log"][j],
                w["ssm_d"][j], w["ssm_norm_g"][j], tl=tl, n_valid=tl if t_pad == t else t)
            y = y[:, :t]
            x = _mix_out(o, y, x, gate_m, ng(1), wb["hyb_w_out"][j])
            ks.append(k_rows.reshape(b, t, H_ATT, 2 * D_HEAD))
            vs.append(v_rows.reshape(b, t, H_ATT, D_V))
            sconvs.append(conv_new)
            ssms.append(ssm_new.reshape(b, H_SSM, SSM_HEADDIM, D_STATE))
        else:
            if past is None:
                past32 = jnp.zeros((b, CONF_HIST, d), F32)
            else:
                past32 = jnp.pad(past[4][j], ((0, 0), (CONF_HIST - (CONF_KW - 1), 0), (0, 0)))
            x, conv_new = _conformer(x, ng(0), scale_m, shift_m, gate_m, ng(1), wb["conf_w_in"][j],
                                     w["conf_b_in"][j], past32, w["conf_dw_w"][j], w["conf_dw_b"][j],
                                     w["conf_ln_g"][j], w["conf_ln_b"][j], wb["conf_w_out"][j],
                                     w["conf_b_out"][j])
            cconvs.append(conv_new)
        x = _ffn(x, ng(2), scale_f, shift_f, gate_f, ng(3), wb["ffn_w_up"][l], wb["ffn_w_down"][l])
    return x, jnp.stack(ks), jnp.stack(vs), jnp.stack(sconvs), jnp.stack(ssms), jnp.stack(cconvs)


def kernel(x_prompt, x_sample, c_prompt, c_sample, cache_attn_k, cache_attn_v, state_ssm_conv, state_ssm, state_conf_conv, ada_w, ada_b, norm_g, ffn_w_up, ffn_w_down, hyb_w_in, attn_lambda, attn_subln_g, ssm_conv_w, ssm_conv_b, ssm_dt_bias, ssm_a_log, ssm_d, ssm_norm_g, hyb_w_out, conf_w_in, conf_b_in, conf_dw_w, conf_dw_b, conf_ln_g, conf_ln_b, conf_w_out, conf_b_out):
    w = dict(norm_g=norm_g, ffn_w_up=ffn_w_up, ffn_w_down=ffn_w_down, hyb_w_in=hyb_w_in,
             attn_lambda=attn_lambda, attn_subln_g=attn_subln_g, ssm_conv_w=ssm_conv_w,
             ssm_conv_b=ssm_conv_b, ssm_dt_bias=ssm_dt_bias, ssm_a_log=ssm_a_log, ssm_d=ssm_d,
             ssm_norm_g=ssm_norm_g, hyb_w_out=hyb_w_out, conf_w_in=conf_w_in, conf_b_in=conf_b_in,
             conf_dw_w=conf_dw_w, conf_dw_b=conf_dw_b, conf_ln_g=conf_ln_g, conf_ln_b=conf_ln_b,
             conf_w_out=conf_w_out, conf_b_out=conf_b_out)
    wb = _prep_weights(w)
    bp, bs = c_prompt.shape[0], c_sample.shape[0]
    rows = -(-(bp + bs) // 16) * 16
    c_all = jnp.concatenate([c_prompt, c_sample, jnp.zeros((rows - bp - bs, D_MODEL), F32)], axis=0)
    mod = _ada(c_all, ada_w, ada_b)
    y_p, k_p, v_p, sconv_p, ssm_p, cconv_p = _trunk(x_prompt, mod[:, 0:bp], None, w, wb)
    past = (cache_attn_k, cache_attn_v, state_ssm_conv, state_ssm, state_conf_conv)
    y_s, k_s, v_s, sconv_s, ssm_s, cconv_s = _trunk(x_sample, mod[:, bp:bp + bs], past, w, wb)
    return (y_p, y_s, k_p, v_p, sconv_p, ssm_p, cconv_p, k_s, v_s, sconv_s, ssm_s, cconv_s)
```
